```python
import math
import jax, jax.numpy as jnp
from jax import lax
import numpy as np

D_MODEL = 1024
BATCH = 8
SEQ = 4096
DEPTH = 1
DEC_BATCH = 128
DEC_SEQ = 4
PAST_LEN = 8192
PAGE_SIZE = 128

D_MIX = D_MODEL
D_REC = D_MIX // 2
D_ATT = D_MIX - D_REC
H_REC = 8
DK_REC = D_REC // H_REC
DV_REC = D_REC // H_REC
H_ATT = 8
DH_ATT = D_ATT // H_ATT
N_EXPERTS = 32
TOP_K = 4
D_FF = D_MODEL
SWIGLU_LIMIT = 7.0
SWIGLU_ALPHA = 1.702
RMS_EPS = 1e-5
HGRN_CHUNK = 64
Q_BLOCK = 128
D_IN = 4 * D_REC + 3 * D_ATT + H_ATT
SPLIT_POINTS = [D_REC, 2 * D_REC, 3 * D_REC, 4 * D_REC,
                4 * D_REC + D_ATT, 4 * D_REC + 2 * D_ATT, 4 * D_REC + 3 * D_ATT]

kernel_name = "hymba_hgrn2_fox_moe_step"


def rmsnorm(x, w):
    xf = x.astype(jnp.float32)
    r = lax.rsqrt(jnp.mean(xf * xf, axis=-1, keepdims=True) + RMS_EPS)
    return (xf * r).astype(x.dtype) * w


def hgrn_lower_bounds(lb_param):
    p = jax.nn.softmax(lb_param.astype(jnp.float32), axis=0)
    return jnp.cumsum(p, axis=0)[:DEPTH]


def hgrn_chunk(S, q, k, v, g):
    C = q.shape[2]
    b = jnp.cumsum(g, axis=2)
    causal = jnp.tril(jnp.ones((C, C), dtype=bool))
    diff = b[:, :, :, None, :] - b[:, :, None, :, :]
    decay = jnp.exp(jnp.where(causal[None, None, :, :, None], diff, -jnp.inf))
    A = jnp.einsum('bhtk,bhsk,bhtsk->bhts', q, k, decay)
    o = jnp.einsum('bhts,bhsv->bhtv', A, v) + jnp.einsum('bhtk,bhkv->bhtv', q * jnp.exp(b), S)
    b_end = b[:, :, -1:, :]
    S_new = jnp.exp(b_end[:, :, 0, :])[..., None] * S + jnp.einsum('bhsk,bhsv->bhkv', k * jnp.exp(b_end - b), v)
    return S_new, o


def hgrn_scan(S0, q, k, v, g):
    B, T = q.shape[0], q.shape[1]
    C = math.gcd(T, HGRN_CHUNK)
    n = T // C

    def to_blocks(a):
        return a.astype(jnp.float32).reshape(B, n, C, a.shape[2], a.shape[3]).transpose(1, 0, 3, 2, 4)

    S, o = lax.scan(lambda s, xs: hgrn_chunk(s, *xs), S0.astype(jnp.float32),
                    (to_blocks(q), to_blocks(k), to_blocks(v), to_blocks(g)))
    o = o.transpose(1, 0, 3, 2, 4).reshape(B, T, H_REC, DV_REC)
    return o, S


def fox_prompt(q, k, v, logf):
    B, T, H, Dh = q.shape
    scale = Dh ** -0.5
    cT = jnp.cumsum(logf, axis=1).transpose(0, 2, 1)
    key_pos = jnp.arange(T)

    def block(i):
        s0 = i * Q_BLOCK
        qb = lax.dynamic_slice_in_dim(q, s0, Q_BLOCK, axis=1)
        cb = lax.dynamic_slice_in_dim(cT, s0, Q_BLOCK, axis=2)
        logits = jnp.einsum('bqhd,bkhd->bhqk', qb, k, preferred_element_type=jnp.float32) * scale
        logits = logits + cb[..., :, None] - cT[..., None, :]
        q_pos = s0 + jnp.arange(Q_BLOCK)
        logits = jnp.where(key_pos[None, :] <= q_pos[:, None], logits, -jnp.inf)
        p = jax.nn.softmax(logits, axis=-1)
        return jnp.einsum('bhqk,bkhd->bqhd', p, v.astype(jnp.float32))

    o = lax.map(block, jnp.arange(T // Q_BLOCK))
    return o.transpose(1, 0, 2, 3, 4).reshape(B, T, H * Dh)


def fox_sample(q, k, v, logf, cache_k, cache_v, cache_logf, page_table):
    B, T, H, Dh = q.shape
    n_pages = page_table.shape[1]
    scale = Dh ** -0.5
    cq = jnp.cumsum(logf, axis=1).transpose(0, 2, 1)
    lf_past = cache_logf[page_table].reshape(B, n_pages * PAGE_SIZE, H).astype(jnp.float32)
    suffix = lax.cumsum(lf_past, axis=1, reverse=True) - lf_past
    suffix_pages = suffix.reshape(B, n_pages, PAGE_SIZE, H).transpose(1, 0, 3, 2)
    s_self = jnp.einsum('bqhd,bkhd->bhqk', q, k, preferred_element_type=jnp.float32) * scale
    s_self = s_self + cq[..., :, None] - cq[..., None, :]
    s_self = jnp.where(jnp.tril(jnp.ones((T, T), dtype=bool)), s_self, -jnp.inf)
    m0 = jnp.max(s_self, axis=-1)
    p0 = jnp.exp(s_self - m0[..., None])
    l0 = jnp.sum(p0, axis=-1)
    acc0 = jnp.einsum('bhqk,bkhd->bhqd', p0, v.astype(jnp.float32))

    def step(carry, xs):
        m, l, acc = carry
        phys, bias = xs
        kp = cache_k[phys]
        vp = cache_v[phys]
        s = jnp.einsum('bqhd,bkhd->bhqk', q, kp, preferred_element_type=jnp.float32) * scale
        s = s + cq[..., :, None] + bias[:, :, None, :]
        m_new = jnp.maximum(m, jnp.max(s, axis=-1))
        corr = jnp.exp(m - m_new)
        p = jnp.exp(s - m_new[..., None])
        l = l * corr + jnp.sum(p, axis=-1)
        acc = acc * corr[..., None] + jnp.einsum('bhqk,bkhd->bhqd', p, vp.astype(jnp.float32))
        return (m_new, l, acc), None

    (m, l, acc), _ = lax.scan(step, (m0, l0, acc0), (page_table.T, suffix_pages))
    o = acc / l[..., None]
    return o.transpose(0, 2, 1, 3).reshape(B, T, H * Dh)


def mixer_block(h, S0, attend, norm_w, w_in, b_fa, lb, gnorm_w, w_out):
    B, T, _ = h.shape
    z = rmsnorm(h, norm_w) @ w_in
    q_r, f_r, i_r, g_r, q_a, k_a, v_a, f_a = jnp.split(z, SPLIT_POINTS, axis=-1)
    fg = lb + (1.0 - lb) * jax.nn.sigmoid(f_r.astype(jnp.float32))
    heads_r = lambda a: a.reshape(B, T, H_REC, DK_REC)
    o_rec, S = hgrn_scan(S0, heads_r(jax.nn.silu(q_r)), heads_r(1.0 - fg),
                         i_r.reshape(B, T, H_REC, DV_REC), heads_r(jnp.log(fg)))
    o_rec = rmsnorm(o_rec, gnorm_w).reshape(B, T, D_REC) * jax.nn.silu(g_r.astype(jnp.float32))
    heads_a = lambda a: a.reshape(B, T, H_ATT, DH_ATT)
    k, v = heads_a(k_a), heads_a(v_a)
    logf = jax.nn.log_sigmoid((f_a + b_fa).astype(jnp.float32))
    o_att = attend(heads_a(q_a), k, v, logf)
    o = jnp.concatenate([o_rec.astype(h.dtype), o_att.astype(h.dtype)], axis=-1)
    return o @ w_out, k, v, logf, S


def moe(h, w_router, b_router, w_gate, b_gate, w_up, b_up, w_down, b_down):
    B, T, D = h.shape
    xt = h.reshape(B * T, D)
    logits = (xt @ w_router + b_router).astype(jnp.float32)
    top_vals, top_idx = lax.top_k(logits, TOP_K)
    gates = jax.nn.softmax(top_vals, axis=-1)
    dense_gate = jnp.sum(jax.nn.one_hot(top_idx, N_EXPERTS, dtype=jnp.float32) * gates[..., None], axis=1)

    def expert(acc, xs):
        wg, bg, wu, bu, wd, bd, gcol = xs
        gate = jnp.minimum(xt @ wg + bg, SWIGLU_LIMIT)
        up = jnp.clip(xt @ wu + bu, -SWIGLU_LIMIT, SWIGLU_LIMIT)
        glu = gate * jax.nn.sigmoid(gate * SWIGLU_ALPHA)
        out = ((up + 1.0) * glu) @ wd + bd
        return acc + gcol[:, None] * out.astype(jnp.float32), None

    acc, _ = lax.scan(expert, jnp.zeros((B * T, D), jnp.float32),
                      (w_gate, b_gate, w_up, b_up, w_down, b_down, dense_gate.T))
    return acc.reshape(B, T, D).astype(h.dtype)


def setup_inputs(seed: int = 0) -> dict:
    key = jax.random.key(seed)
    ks = jax.random.split(key, 24)
    n_pages = PAST_LEN // PAGE_SIZE
    n_used = DEC_BATCH * n_pages
    n_pool = n_used + n_used // 4
    nrm = lambda k, shape, s=1.0: s * jax.random.normal(k, shape, jnp.float32)
    perm = jax.random.permutation(ks[6], n_pool)
    page_table = perm[:n_used].reshape(DEC_BATCH, n_pages).astype(jnp.int32)
    return {
        "x_prompt": nrm(ks[0], (BATCH, SEQ, D_MODEL)),
        "x_sample": nrm(ks[1], (DEC_BATCH, DEC_SEQ, D_MODEL)),
        "cache_k": nrm(ks[2], (DEPTH, n_pool, PAGE_SIZE, H_ATT, DH_ATT)),
        "cache_v": nrm(ks[3], (DEPTH, n_pool, PAGE_SIZE, H_ATT, DH_ATT)),
        "cache_logf": jax.nn.log_sigmoid(8.0 + nrm(ks[4], (DEPTH, n_pool, PAGE_SIZE, H_ATT), 0.5)),
        "state_hgrn": nrm(ks[5], (DEPTH, DEC_BATCH, H_REC, DK_REC, DV_REC), 0.3),
        "page_table": page_table,
        "norm_mix_w": 1.0 + nrm(ks[7], (DEPTH, D_MODEL), 0.05),
        "w_in": nrm(ks[8], (DEPTH, D_MODEL, D_IN), D_MODEL ** -0.5),
        "b_fa": 4.0 + nrm(ks[9], (DEPTH, H_ATT), 0.5),
        "lb_param": nrm(ks[10], (DEPTH + 1, D_REC), 0.5),
        "gnorm_w": 1.0 + nrm(ks[11], (DV_REC,), 0.05),
        "w_out": nrm(ks[12], (DEPTH, D_MIX, D_MODEL), D_MIX ** -0.5),
        "norm_ffn_w": 1.0 + nrm(ks[13], (DEPTH, D_MODEL), 0.05),
        "w_router": nrm(ks[14], (DEPTH, D_MODEL, N_EXPERTS), D_MODEL ** -0.5),
        "b_router": nrm(ks[15], (DEPTH, N_EXPERTS), 0.01),
        "w_gate": nrm(ks[16], (DEPTH, N_EXPERTS, D_MODEL, D_FF), D_MODEL ** -0.5),
        "b_gate": nrm(ks[17], (DEPTH, N_EXPERTS, D_FF), 0.02),
        "w_up": nrm(ks[18], (DEPTH, N_EXPERTS, D_MODEL, D_FF), D_MODEL ** -0.5),
        "b_up": nrm(ks[19], (DEPTH, N_EXPERTS, D_FF), 0.02),
        "w_down": nrm(ks[20], (DEPTH, N_EXPERTS, D_FF, D_MODEL), D_FF ** -0.5),
        "b_down": nrm(ks[21], (DEPTH, N_EXPERTS, D_MODEL), 0.02),
        "norm_final_w": 1.0 + nrm(ks[22], (D_MODEL,), 0.05),
    }


def reference(x_prompt, x_sample, cache_k, cache_v, cache_logf, state_hgrn, page_table,
              norm_mix_w, w_in, b_fa, lb_param, gnorm_w, w_out, norm_ffn_w,
              w_router, b_router, w_gate, b_gate, w_up, b_up, w_down, b_down, norm_final_w):
    lbs = hgrn_lower_bounds(lb_param)
    hp, hs = x_prompt, x_sample
    kp, vp, lfp, sp, ksm, vsm, lfs, ss = [], [], [], [], [], [], [], []
    for l in range(DEPTH):
        mix_args = (norm_mix_w[l], w_in[l], b_fa[l], lbs[l], gnorm_w, w_out[l])
        moe_args = (w_router[l], b_router[l], w_gate[l], b_gate[l], w_up[l], b_up[l], w_down[l], b_down[l])
        S0 = jnp.zeros((hp.shape[0], H_REC, DK_REC, DV_REC), jnp.float32)
        dp, k_p, v_p, lf_p, S_p = mixer_block(hp, S0, fox_prompt, *mix_args)
        hp = hp + dp
        hp = hp + moe(rmsnorm(hp, norm_ffn_w[l]), *moe_args)
        attend_s = lambda q, k, v, lf: fox_sample(q, k, v, lf, cache_k[l], cache_v[l], cache_logf[l], page_table)
        ds, k_s, v_s, lf_s, S_s = mixer_block(hs, state_hgrn[l], attend_s, *mix_args)
        hs = hs + ds
        hs = hs + moe(rmsnorm(hs, norm_ffn_w[l]), *moe_args)
        kp.append(k_p); vp.append(v_p); lfp.append(lf_p); sp.append(S_p)
        ksm.append(k_s); vsm.append(v_s); lfs.append(lf_s); ss.append(S_s)
    y_prompt = rmsnorm(hp, norm_final_w)
    y_sample = rmsnorm(hs, norm_final_w)
    return (y_prompt, y_sample,
            jnp.stack(kp), jnp.stack(vp), jnp.stack(lfp), jnp.stack(sp),
            jnp.stack(ksm), jnp.stack(vsm), jnp.stack(lfs), jnp.stack(ss))
```

```python
import functools

import jax
import jax.numpy as jnp
from jax import lax
from jax.experimental import pallas as pl
from jax.experimental.pallas import tpu as pltpu

D_MODEL = 1024
D_REC = 512
D_ATT = 512
N_HEADS = 8
D_HEAD = 64
N_EXPERTS = 32
TOP_K = 4
D_FF = 1024
PAGE_SIZE = 128
SWIGLU_LIMIT = 7.0
SWIGLU_ALPHA = 1.702
RMS_EPS = 1e-5

LANES = 128
NEG_BIG = -1e30
MIB = 1024 * 1024

F32 = jnp.float32
BF16 = jnp.bfloat16
HIGHEST = lax.Precision.HIGHEST

NT_DIMS = (((1,), (1,)), ((), ()))
TN_DIMS = (((0,), (0,)), ((), ()))


def _log_sigmoid(x):
    return jnp.minimum(x, 0.0) - jnp.log1p(jnp.exp(-jnp.abs(x)))


def _silu(x):
    return x * jax.nn.sigmoid(x)


def _params(sem, vmem_mib):
    return pltpu.CompilerParams(dimension_semantics=sem, vmem_limit_bytes=vmem_mib * MIB)


def _inproj_kernel(x_ref, nw_ref, wrec_ref, watt_ref, wfa_ref, wfat_ref, bfa_ref, bfat_ref,
                   zrec_ref, q_ref, k_ref, v_ref, kb_ref, vb_ref, lf_ref, lft_ref):
    x = x_ref[...]
    r = lax.rsqrt(jnp.mean(x * x, axis=-1, keepdims=True) + RMS_EPS)
    xn = ((x * r) * nw_ref[...]).astype(BF16)
    zrec_ref[...] = jnp.dot(xn, wrec_ref[...], preferred_element_type=F32)
    za = jnp.dot(xn, watt_ref[...], preferred_element_type=F32)
    q_ref[...] = (za[:, :D_ATT] * (D_HEAD ** -0.5)).astype(BF16)
    k = za[:, D_ATT:2 * D_ATT]
    v = za[:, 2 * D_ATT:]
    k_ref[...] = k
    v_ref[...] = v
    kb_ref[...] = k.astype(BF16)
    vb_ref[...] = v.astype(BF16)
    fa = jnp.dot(xn, wfa_ref[...], preferred_element_type=F32)
    lf_ref[...] = _log_sigmoid(fa[:, :N_HEADS] + bfa_ref[...])
    fat = lax.dot_general(wfat_ref[...], xn, NT_DIMS, preferred_element_type=F32)
    lft_ref[...] = _log_sigmoid(fat[:N_HEADS, :] + bfat_ref[...])


def _inproj(x, nw, wrec, watt, wfa, wfat, bfa, bfat, tm):
    n = x.shape[0]
    full = lambda a: pl.BlockSpec(a.shape, lambda i: (0,) * a.ndim)
    row = lambda w: pl.BlockSpec((tm, w), lambda i: (i, 0))
    return pl.pallas_call(
        _inproj_kernel,
        grid=(n // tm,),
        in_specs=[row(D_MODEL), full(nw), full(wrec), full(watt), full(wfa), full(wfat), full(bfa), full(bfat)],
        out_specs=[row(4 * D_REC), row(D_ATT), row(D_ATT), row(D_ATT), row(D_ATT), row(D_ATT), row(N_HEADS),
                   pl.BlockSpec((N_HEADS, tm), lambda i: (0, i))],
        out_shape=[jax.ShapeDtypeStruct((n, 4 * D_REC), F32), jax.ShapeDtypeStruct((n, D_ATT), BF16),
                   jax.ShapeDtypeStruct((n, D_ATT), F32), jax.ShapeDtypeStruct((n, D_ATT), F32),
                   jax.ShapeDtypeStruct((n, D_ATT), BF16), jax.ShapeDtypeStruct((n, D_ATT), BF16),
                   jax.ShapeDtypeStruct((n, N_HEADS), F32), jax.ShapeDtypeStruct((N_HEADS, n), F32)],
        compiler_params=_params(("parallel",), 52),
        name="inproj",
    )(x, nw, wrec, watt, wfa, wfat, bfa, bfat)


def _cumsum_kernel(lf_ref, lft_ref, tri_ref, triu_ref, c_ref, ct_ref, *, tb):
    t = lf_ref.shape[1]
    tri = tri_ref[...]
    triu = triu_ref[...]
    carry = jnp.zeros((1, N_HEADS), F32)
    carry_t = jnp.zeros((N_HEADS, 1), F32)
    for j in range(t // tb):
        sl = slice(j * tb, (j + 1) * tb)
        cb = jnp.dot(tri, lf_ref[0, sl, :], precision=HIGHEST, preferred_element_type=F32) + carry
        c_ref[0, sl, :] = cb
        carry = cb[tb - 1:tb, :]
        cbt = jnp.dot(lft_ref[:, sl], triu, precision=HIGHEST, preferred_element_type=F32) + carry_t
        ct_ref[:, sl] = cbt
        carry_t = cbt[:, tb - 1:tb]


def _cumsum_logf(lf, lft, batch, t):
    tb = min(t, 512)
    idx = jnp.arange(tb)
    tri = (idx[:, None] >= idx[None, :]).astype(F32)
    return pl.pallas_call(
        functools.partial(_cumsum_kernel, tb=tb),
        grid=(batch,),
        in_specs=[pl.BlockSpec((1, t, N_HEADS), lambda b: (b, 0, 0)),
                  pl.BlockSpec((N_HEADS, t), lambda b: (0, b)),
                  pl.BlockSpec((tb, tb), lambda b: (0, 0)),
                  pl.BlockSpec((tb, tb), lambda b: (0, 0))],
        out_specs=[pl.BlockSpec((1, t, N_HEADS), lambda b: (b, 0, 0)),
                   pl.BlockSpec((N_HEADS, t), lambda b: (0, b))],
        out_shape=[jax.ShapeDtypeStruct((batch, t, N_HEADS), F32),
                   jax.ShapeDtypeStruct((N_HEADS, batch * t), F32)],
        compiler_params=_params(("parallel",), 32),
        name="cumsum_logf",
    )(lf.reshape(batch, t, N_HEADS), lft, tri, tri.T)


def _fox_prompt_kernel(q_ref, k_ref, v_ref, c_ref, ct_ref, o_ref, m_sc, l_sc, acc_sc, *, tq):
    p = pl.program_id(1)
    i = pl.program_id(2)
    q = q_ref[...]
    lane = lax.broadcasted_iota(jnp.int32, (tq, LANES), 1)
    lane8 = lax.broadcasted_iota(jnp.int32, (tq, N_HEADS), 1)
    row = lax.broadcasted_iota(jnp.int32, (tq, tq), 0)
    col = lax.broadcasted_iota(jnp.int32, (tq, tq), 1)
    cblk = c_ref[0]
    qs, cqs = [], []
    for hh in range(2):
        qs.append(jnp.where((lane // D_HEAD) == hh, q, jnp.zeros_like(q)))
        cqs.append(jnp.sum(jnp.where(lane8 == 2 * p + hh, cblk, 0.0), axis=-1, keepdims=True))
    m_sc[...] = jnp.full(m_sc.shape, NEG_BIG, F32)
    l_sc[...] = jnp.zeros(l_sc.shape, F32)
    acc_sc[...] = jnp.zeros(acc_sc.shape, F32)

    def step(j, masked):
        off = pl.multiple_of(j * tq, tq)
        kb = k_ref[pl.ds(off, tq), :]
        vb = v_ref[pl.ds(off, tq), :]
        for hh in range(2):
            s = lax.dot_general(qs[hh], kb, NT_DIMS, preferred_element_type=F32)
            ck = ct_ref[0, hh:hh + 1, pl.ds(off, tq)]
            s = s + (cqs[hh] - ck)
            if masked:
                s = jnp.where(col <= row, s, NEG_BIG)
            m_prev = m_sc[hh]
            m_new = jnp.maximum(m_prev, jnp.max(s, axis=-1, keepdims=True))
            alpha = jnp.exp(m_prev - m_new)
            pr = jnp.exp(s - m_new)
            l_sc[hh] = alpha * l_sc[hh] + jnp.sum(pr, axis=-1, keepdims=True)
            acc_sc[hh] = alpha * acc_sc[hh] + jnp.dot(pr.astype(BF16), vb, preferred_element_type=F32)
            m_sc[hh] = m_new

    def body(j, carry):
        step(j, False)
        return carry

    lax.fori_loop(0, i, body, 0)
    step(i, True)
    o = jnp.where(lane < D_HEAD, acc_sc[0] / l_sc[0], acc_sc[1] / l_sc[1])
    o_ref[...] = o.astype(o_ref.dtype)


def _fox_prompt(q, kb, vb, c, ct, batch, t, tq):
    nq = t // tq
    ct4 = ct.reshape(N_HEADS // 2, 2, batch * t)
    return pl.pallas_call(
        functools.partial(_fox_prompt_kernel, tq=tq),
        grid=(batch, N_HEADS // 2, nq),
        in_specs=[pl.BlockSpec((tq, LANES), lambda b, p, i: (b * nq + i, p)),
                  pl.BlockSpec((t, LANES), lambda b, p, i: (b, p)),
                  pl.BlockSpec((t, LANES), lambda b, p, i: (b, p)),
                  pl.BlockSpec((1, tq, N_HEADS), lambda b, p, i: (b, i, 0)),
                  pl.BlockSpec((1, 2, t), lambda b, p, i: (p, 0, b))],
        out_specs=pl.BlockSpec((tq, LANES), lambda b, p, i: (b * nq + i, p)),
        out_shape=jax.ShapeDtypeStruct((batch * t, D_ATT), BF16),
        scratch_shapes=[pltpu.VMEM((2, tq, 1), F32), pltpu.VMEM((2, tq, 1), F32), pltpu.VMEM((2, tq, LANES), F32)],
        compiler_params=_params(("parallel", "parallel", "parallel"), 32),
        name="fox_prompt",
    )(q, kb, vb, c, ct4)


def _hgrn_chunk(zq, zf, zi, lb, gw, tri, s_sc, o_sc, *, chunk, sub, n_valid):
    fg = lb + (1.0 - lb) * jax.nn.sigmoid(zf)
    kk = 1.0 - fg
    g = jnp.log(fg)
    if n_valid < chunk:
        valid = lax.broadcasted_iota(jnp.int32, (chunk, D_REC), 0) < n_valid
        kk = jnp.where(valid, kk, 0.0)
        g = jnp.where(valid, g, 0.0)
    q = _silu(zq)
    b = jnp.dot(tri, g, precision=HIGHEST, preferred_element_type=F32)
    b_end = b[chunk - 1:chunk, :]
    qhat = (q * jnp.exp(b)).astype(BF16)
    khat = (kk * jnp.exp(b_end - b)).astype(BF16)
    vb = zi.astype(BF16)
    e_end = jnp.exp(b_end)
    s_old = [s_sc[h] for h in range(N_HEADS)]
    s_old_b = [s.astype(BF16) for s in s_old]
    for i in range(chunk // sub):
        r0, r1 = i * sub, (i + 1) * sub
        rho = b[r0 + sub // 2:r0 + sub // 2 + 1, :]
        qt = (q[r0:r1] * jnp.exp(b[r0:r1] - rho)).astype(BF16)
        kt = (kk[:r1] * jnp.exp(rho - b[:r1])).astype(BF16)
        row = lax.broadcasted_iota(jnp.int32, (sub, r1), 0) + r0
        col = lax.broadcasted_iota(jnp.int32, (sub, r1), 1)
        for h in range(N_HEADS):
            hs = slice(h * D_HEAD, (h + 1) * D_HEAD)
            a = lax.dot_general(qt[:, hs], kt[:, hs], NT_DIMS, preferred_element_type=F32)
            a = jnp.where(col <= row, a, 0.0).astype(BF16)
            o = (jnp.dot(a, vb[:r1, hs], preferred_element_type=F32)
                 + jnp.dot(qhat[r0:r1, hs], s_old_b[h], preferred_element_type=F32))
            ms = jnp.mean(o * o, axis=-1, keepdims=True)
            o_sc[r0:r1, hs] = (o * lax.rsqrt(ms + RMS_EPS)) * gw
    eye = (lax.broadcasted_iota(jnp.int32, (D_HEAD, D_HEAD), 0)
           == lax.broadcasted_iota(jnp.int32, (D_HEAD, D_HEAD), 1))
    for h in range(N_HEADS):
        hs = slice(h * D_HEAD, (h + 1) * D_HEAD)
        e_col = jnp.sum(jnp.where(eye, jnp.broadcast_to(e_end[:, hs], (D_HEAD, D_HEAD)), 0.0), axis=1, keepdims=True)
        s_sc[h] = e_col * s_old[h] + lax.dot_general(khat[:, hs], vb[:, hs], TN_DIMS, preferred_element_type=F32)


def _hgrn_prompt_kernel(z_ref, lb_ref, gw_ref, tri_ref, o_ref, s_ref, s_sc, o_sc, *, chunk, sub):
    j = pl.program_id(1)

    @pl.when(j == 0)
    def _():
        s_sc[...] = jnp.zeros(s_sc.shape, F32)

    z = z_ref[...]
    _hgrn_chunk(z[:, :D_REC], z[:, D_REC:2 * D_REC], z[:, 2 * D_REC:3 * D_REC], lb_ref[...], gw_ref[...],
                tri_ref[...], s_sc, o_sc, chunk=chunk, sub=sub, n_valid=chunk)
    o_ref[...] = (o_sc[...] * _silu(z[:, 3 * D_REC:])).astype(o_ref.dtype)

    @pl.when(j == pl.num_programs(1) - 1)
    def _():
        s_ref[0] = s_sc[...]


def _hgrn_prompt(zrec, lb, gw, batch, t, chunk, sub):
    nc = t // chunk
    idx = jnp.arange(chunk)
    tri = (idx[:, None] >= idx[None, :]).astype(F32)
    return pl.pallas_call(
        functools.partial(_hgrn_prompt_kernel, chunk=chunk, sub=sub),
        grid=(batch, nc),
        in_specs=[pl.BlockSpec((chunk, 4 * D_REC), lambda b, j: (b * nc + j, 0)),
                  pl.BlockSpec((1, D_REC), lambda b, j: (0, 0)),
                  pl.BlockSpec((1, D_HEAD), lambda b, j: (0, 0)),
                  pl.BlockSpec((chunk, chunk), lambda b, j: (0, 0))],
        out_specs=[pl.BlockSpec((chunk, D_REC), lambda b, j: (b * nc + j, 0)),
                   pl.BlockSpec((1, N_HEADS, D_HEAD, D_HEAD), lambda b, j: (b, 0, 0, 0))],
        out_shape=[jax.ShapeDtypeStruct((batch * t, D_REC), BF16),
                   jax.ShapeDtypeStruct((batch, N_HEADS, D_HEAD, D_HEAD), F32)],
        scratch_shapes=[pltpu.VMEM((N_HEADS, D_HEAD, D_HEAD), F32), pltpu.VMEM((chunk, D_REC), F32)],
        compiler_params=_params(("parallel", "arbitrary"), 32),
        name="hgrn_prompt",
    )(zrec, lb, gw, tri)


def _hgrn_sample_kernel(z_ref, s0_ref, lb_ref, gw_ref, tri_ref, o_ref, s_ref, s_sc, o_sc, *, seqs, t, chunk):
    lb = lb_ref[...]
    gw = gw_ref[...]
    tri = tri_ref[...]
    pad = jnp.zeros((chunk - t, D_REC), F32)
    for s in range(seqs):
        z = z_ref[s]
        s_sc[...] = s0_ref[s]
        zq, zf, zi = (jnp.concatenate([z[:, u * D_REC:(u + 1) * D_REC], pad], axis=0) for u in range(3))
        _hgrn_chunk(zq, zf, zi, lb, gw, tri, s_sc, o_sc, chunk=chunk, sub=chunk, n_valid=t)
        o_ref[s] = (o_sc[:t, :] * _silu(z[:, 3 * D_REC:])).astype(o_ref.dtype)
        s_ref[s] = s_sc[...]


def _hgrn_sample(zrec, s0, lb, gw, batch, t, seqs):
    chunk = 8
    idx = jnp.arange(chunk)
    tri = (idx[:, None] >= idx[None, :]).astype(F32)
    return pl.pallas_call(
        functools.partial(_hgrn_sample_kernel, seqs=seqs, t=t, chunk=chunk),
        grid=(batch // seqs,),
        in_specs=[pl.BlockSpec((seqs, t, 4 * D_REC), lambda b: (b, 0, 0)),
                  pl.BlockSpec((seqs, N_HEADS, D_HEAD, D_HEAD), lambda b: (b, 0, 0, 0)),
                  pl.BlockSpec((1, D_REC), lambda b: (0, 0)),
                  pl.BlockSpec((1, D_HEAD), lambda b: (0, 0)),
                  pl.BlockSpec((chunk, chunk), lambda b: (0, 0))],
        out_specs=[pl.BlockSpec((seqs, t, D_REC), lambda b: (b, 0, 0)),
                   pl.BlockSpec((seqs, N_HEADS, D_HEAD, D_HEAD), lambda b: (b, 0, 0, 0))],
        out_shape=[jax.ShapeDtypeStruct((batch, t, D_REC), BF16),
                   jax.ShapeDtypeStruct((batch, N_HEADS, D_HEAD, D_HEAD), F32)],
        scratch_shapes=[pltpu.VMEM((N_HEADS, D_HEAD, D_HEAD), F32), pltpu.VMEM((chunk, D_REC), F32)],
        compiler_params=_params(("parallel",), 32),
        name="hgrn_sample",
    )(zrec.reshape(batch, t, 4 * D_REC), s0, lb, gw, tri)


def _fox_sample_kernel(pt_ref, q_ref, kn_ref, vn_ref, lft_ref, gs_ref, *rest, t, ppb):
    k_refs = rest[:ppb]
    v_refs = rest[ppb:2 * ppb]
    lf_refs = rest[2 * ppb:3 * ppb]
    o_ref = rest[3 * ppb]
    m_sc, l_sc, acc_sc, carry_sc, cq_sc = rest[3 * ppb + 1:]
    del pt_ref
    j = pl.program_id(1)
    nrow = t * N_HEADS
    lane = lax.broadcasted_iota(jnp.int32, (nrow, D_ATT), 1)
    rowi = lax.broadcasted_iota(jnp.int32, (nrow, D_ATT), 0)
    head_lanes = (lane // D_HEAD) == (rowi % N_HEADS)
    q = q_ref[0]
    qbd = jnp.concatenate([jnp.broadcast_to(q[u:u + 1, :], (N_HEADS, D_ATT)) for u in range(t)], axis=0)
    qbd = jnp.where(head_lanes, qbd, jnp.zeros_like(qbd))

    @pl.when(j == 0)
    def _():
        lft = lft_ref[0]
        cols = [lft[:, 0:1]]
        for u in range(1, t):
            cols.append(cols[-1] + lft[:, u:u + 1])
        cq_t = jnp.concatenate(cols, axis=1)
        cq_col = jnp.concatenate(cols, axis=0)
        cq_sc[...] = cq_col
        zpad = jnp.zeros((N_HEADS - t, D_ATT), F32)
        kn = jnp.concatenate([kn_ref[0], zpad], axis=0).astype(BF16)
        vn = jnp.concatenate([vn_ref[0], zpad], axis=0).astype(BF16)
        cq_t8 = jnp.concatenate([cq_t, jnp.zeros((N_HEADS, N_HEADS - t), F32)], axis=1)
        s = lax.dot_general(qbd, kn, NT_DIMS, preferred_element_type=F32)
        s = s + cq_col - jnp.concatenate([cq_t8] * t, axis=0)
        r2 = lax.broadcasted_iota(jnp.int32, (nrow, N_HEADS), 0) // N_HEADS
        c2 = lax.broadcasted_iota(jnp.int32, (nrow, N_HEADS), 1)
        s = jnp.where(c2 <= r2, s, NEG_BIG)
        m0 = jnp.max(s, axis=-1, keepdims=True)
        p0 = jnp.exp(s - m0)
        m_sc[...] = m0
        l_sc[...] = jnp.sum(p0, axis=-1, keepdims=True)
        acc_sc[...] = jnp.dot(p0.astype(BF16), vn, preferred_element_type=F32)
        carry_sc[...] = jnp.zeros(carry_sc.shape, F32)

    lf_all = jnp.concatenate([r[0] for r in lf_refs], axis=0)
    insuf = jnp.dot(lf_all, gs_ref[...], precision=HIGHEST, preferred_element_type=F32)
    carry = carry_sc[...]
    biases = []
    for u in range(ppb):
        biases.append(insuf[u * N_HEADS:(u + 1) * N_HEADS, :] + carry)
        carry = carry + jnp.sum(lf_refs[u][0], axis=-1, keepdims=True)
    carry_sc[...] = carry
    bias = jnp.concatenate(biases, axis=1)
    bias = jnp.concatenate([bias] * t, axis=0) + cq_sc[...]
    kcat = jnp.concatenate([r[0].astype(BF16) for r in k_refs], axis=0)
    vcat = jnp.concatenate([r[0].astype(BF16) for r in v_refs], axis=0)
    s = lax.dot_general(qbd, kcat, NT_DIMS, preferred_element_type=F32) + bias
    m_prev = m_sc[...]
    m_new = jnp.maximum(m_prev, jnp.max(s, axis=-1, keepdims=True))
    alpha = jnp.exp(m_prev - m_new)
    pr = jnp.exp(s - m_new)
    l_sc[...] = alpha * l_sc[...] + jnp.sum(pr, axis=-1, keepdims=True)
    acc_sc[...] = alpha * acc_sc[...] + jnp.dot(pr.astype(BF16), vcat, preferred_element_type=F32)
    m_sc[...] = m_new

    @pl.when(j == pl.num_programs(1) - 1)
    def _():
        o = jnp.where(head_lanes, acc_sc[...] / l_sc[...], 0.0)
        o_ref[0] = jnp.sum(o.reshape(t, N_HEADS, D_ATT), axis=1).astype(o_ref.dtype)


def _fox_sample(page_table, q, kn, vn, lft, cache_k, cache_v, cache_lft, batch, t, ppb):
    n_pages = page_table.shape[1]
    nsteps = n_pages // ppb
    idx = jnp.arange(PAGE_SIZE)
    gs = (idx[:, None] > idx[None, :]).astype(F32)
    seq = lambda w: pl.BlockSpec((1, t, w), lambda b, j, pt: (b, 0, 0))

    def page(u, shape):
        return pl.BlockSpec((1,) + shape, lambda b, j, pt: (pt[b, n_pages - 1 - (j * ppb + u)], 0, 0))

    in_specs = ([seq(D_ATT), seq(D_ATT), seq(D_ATT),
                 pl.BlockSpec((1, N_HEADS, t), lambda b, j, pt: (b, 0, 0)),
                 pl.BlockSpec((PAGE_SIZE, PAGE_SIZE), lambda b, j, pt: (0, 0))]
                + [page(u, (PAGE_SIZE, D_ATT)) for u in range(ppb)]
                + [page(u, (PAGE_SIZE, D_ATT)) for u in range(ppb)]
                + [page(u, (N_HEADS, PAGE_SIZE)) for u in range(ppb)])
    nrow = t * N_HEADS
    grid_spec = pltpu.PrefetchScalarGridSpec(
        num_scalar_prefetch=1,
        grid=(batch, nsteps),
        in_specs=in_specs,
        out_specs=pl.BlockSpec((1, t, D_ATT), lambda b, j, pt: (b, 0, 0)),
        scratch_shapes=[pltpu.VMEM((nrow, 1), F32), pltpu.VMEM((nrow, 1), F32), pltpu.VMEM((nrow, D_ATT), F32),
                        pltpu.VMEM((N_HEADS, 1), F32), pltpu.VMEM((nrow, 1), F32)],
    )
    return pl.pallas_call(
        functools.partial(_fox_sample_kernel, t=t, ppb=ppb),
        grid_spec=grid_spec,
        out_shape=jax.ShapeDtypeStruct((batch, t, D_ATT), BF16),
        compiler_params=_params(("parallel", "arbitrary"), 40),
        name="fox_sample",
    )(page_table, q.reshape(batch, t, D_ATT), kn.reshape(batch, t, D_ATT), vn.reshape(batch, t, D_ATT), lft, gs,
      *([cache_k] * ppb), *([cache_v] * ppb), *([cache_lft] * ppb))


def _outproj_router_kernel(x_ref, orec_ref, oatt_ref, wo_ref, nw_ref, wr_ref, br_ref, cnt0_ref, tril_ref,
                           h_ref, xn_ref, idx_ref, gate_ref, rank_ref, cnt_ref, base_sc):
    i = pl.program_id(0)

    @pl.when(i == 0)
    def _():
        base_sc[...] = cnt0_ref[...]

    tm = x_ref.shape[0]
    h = (x_ref[...]
         + jnp.dot(orec_ref[...], wo_ref[:D_REC, :], preferred_element_type=F32)
         + jnp.dot(oatt_ref[...], wo_ref[D_REC:, :], preferred_element_type=F32))
    h_ref[...] = h
    r = lax.rsqrt(jnp.mean(h * h, axis=-1, keepdims=True) + RMS_EPS)
    xn = (h * r) * nw_ref[...]
    xn_ref[...] = xn
    logits = jnp.dot(xn, wr_ref[...], precision=HIGHEST, preferred_element_type=F32) + br_ref[...]
    lane = lax.broadcasted_iota(jnp.int32, (tm, LANES), 1)
    vals, hots = [], []
    idx_out = jnp.zeros((tm, LANES), jnp.int32)
    for k in range(TOP_K):
        m = jnp.max(logits, axis=-1, keepdims=True)
        sel = jnp.min(jnp.where(logits == m, lane, LANES), axis=-1, keepdims=True)
        hot = lane == sel
        vals.append(m)
        hots.append(hot)
        idx_out = jnp.where(lane == k, sel, idx_out)
        logits = jnp.where(hot, -jnp.inf, logits)
    idx_ref[...] = idx_out
    es = [jnp.exp(v - vals[0]) for v in vals]
    denom = es[0] + es[1] + es[2] + es[3]
    gate_out = jnp.zeros((tm, LANES), F32)
    for k in range(TOP_K):
        gate_out = jnp.where(lane == k, es[k] / denom, gate_out)
    gate_ref[...] = gate_out
    base = base_sc[...]
    rank_out = jnp.zeros((tm, LANES), jnp.int32)
    for k in range(TOP_K):
        hot_f = hots[k].astype(F32)
        before = jnp.dot(tril_ref[...], hot_f.astype(BF16), preferred_element_type=F32) + base
        rank = jnp.sum(jnp.where(hots[k], before, 0.0), axis=-1, keepdims=True)
        rank_out = jnp.where(lane == k, rank.astype(jnp.int32), rank_out)
        base = base + jnp.sum(hot_f, axis=0, keepdims=True)
    rank_ref[...] = rank_out
    base_sc[...] = base
    cnt_ref[...] = base


def _outproj_router(x, orec, oatt, wo, nw, wr, br, cnt0, tm):
    n = x.shape[0]
    idx = jnp.arange(tm)
    tril = (idx[:, None] > idx[None, :]).astype(BF16)
    full = lambda a: pl.BlockSpec(a.shape, lambda i: (0,) * a.ndim)
    row = lambda w: pl.BlockSpec((tm, w), lambda i: (i, 0))
    return pl.pallas_call(
        _outproj_router_kernel,
        grid=(n // tm,),
        in_specs=[row(D_MODEL), row(D_REC), row(D_ATT), full(wo), full(nw), full(wr), full(br), full(cnt0), full(tril)],
        out_specs=[row(D_MODEL), row(D_MODEL), row(LANES), row(LANES), row(LANES),
                   pl.BlockSpec((1, LANES), lambda i: (0, 0))],
        out_shape=[jax.ShapeDtypeStruct((n, D_MODEL), F32), jax.ShapeDtypeStruct((n, D_MODEL), F32),
                   jax.ShapeDtypeStruct((n, LANES), jnp.int32), jax.ShapeDtypeStruct((n, LANES), F32),
                   jax.ShapeDtypeStruct((n, LANES), jnp.int32), jax.ShapeDtypeStruct((1, LANES), F32)],
        scratch_shapes=[pltpu.VMEM((1, LANES), F32)],
        compiler_params=_params(("arbitrary",), 40),
        name="outproj_router",
    )(x, orec, oatt, wo, nw, wr, br, cnt0, tril)


def _dispatch_kernel(dest_ref, xn_ref, xs_in_ref, xs_ref, sem):
    del xs_in_ref
    tm = xn_ref.shape[0]

    def row_copy(r, k):
        return pltpu.make_async_copy(xn_ref.at[pl.ds(r, 1), :],
                                     xs_ref.at[pl.ds(dest_ref[0, 0, r * TOP_K + k], 1), :], sem)

    def issue(r, carry):
        for k in range(TOP_K):
            row_copy(r, k).start()
        return carry

    lax.fori_loop(0, tm, issue, 0)
    for k in range(TOP_K):
        pltpu.make_async_copy(xn_ref, xs_ref.at[pl.ds(0, tm), :], sem).wait()


def _dispatch(dest, xn, xs, tm):
    n = xn.shape[0]
    return pl.pallas_call(
        _dispatch_kernel,
        grid=(n // tm,),
        in_specs=[pl.BlockSpec((1, 1, tm * TOP_K), lambda i: (i, 0, 0), memory_space=pltpu.SMEM),
                  pl.BlockSpec((tm, D_MODEL), lambda i: (i, 0)),
                  pl.BlockSpec(memory_space=pl.ANY)],
        out_specs=pl.BlockSpec(memory_space=pl.ANY),
        out_shape=jax.ShapeDtypeStruct(xs.shape, xs.dtype),
        scratch_shapes=[pltpu.SemaphoreType.DMA(())],
        input_output_aliases={2: 0},
        compiler_params=_params(("arbitrary",), 32),
        name="moe_dispatch",
    )(dest.reshape(n // tm, 1, tm * TOP_K), xn, xs)


def _expert_mlp_kernel(te_ref, tv_ref, x_ref, wg_ref, bg_ref, wu_ref, bu_ref, wd_ref, bd_ref, y_ref):
    i = pl.program_id(0)
    del te_ref

    @pl.when(tv_ref[i] != 0)
    def _():
        x = x_ref[...].astype(BF16)
        gate = jnp.minimum(jnp.dot(x, wg_ref[0], preferred_element_type=F32) + bg_ref[0], SWIGLU_LIMIT)
        up = jnp.clip(jnp.dot(x, wu_ref[0], preferred_element_type=F32) + bu_ref[0], -SWIGLU_LIMIT, SWIGLU_LIMIT)
        glu = gate * jax.nn.sigmoid(gate * SWIGLU_ALPHA)
        mid = ((up + 1.0) * glu).astype(BF16)
        y_ref[...] = jnp.dot(mid, wd_ref[0], preferred_element_type=F32) + bd_ref[0]

    @pl.when(tv_ref[i] == 0)
    def _():
        y_ref[...] = jnp.zeros(y_ref.shape, y_ref.dtype)


def _expert_mlp(tile_expert, tile_valid, xs, wg, bg, wu, bu, wd, bd, tm):
    p = xs.shape[0]
    wspec = lambda: pl.BlockSpec((1, D_MODEL, D_FF), lambda i, te, tv: (te[i], 0, 0))
    bspec = lambda: pl.BlockSpec((1, 1, D_FF), lambda i, te, tv: (te[i], 0, 0))
    grid_spec = pltpu.PrefetchScalarGridSpec(
        num_scalar_prefetch=2,
        grid=(p // tm,),
        in_specs=[pl.BlockSpec((tm, D_MODEL), lambda i, te, tv: (i, 0)),
                  wspec(), bspec(), wspec(), bspec(), wspec(), bspec()],
        out_specs=pl.BlockSpec((tm, D_MODEL), lambda i, te, tv: (i, 0)),
    )
    return pl.pallas_call(
        _expert_mlp_kernel,
        grid_spec=grid_spec,
        out_shape=jax.ShapeDtypeStruct((p, D_MODEL), F32),
        compiler_params=_params(("parallel",), 48),
        name="expert_mlp",
    )(tile_expert, tile_valid, xs, wg, bg, wu, bu, wd, bd)


def _combine_kernel(dest_ref, gate_ref, h_ref, nw_ref, ys_ref, y_ref, buf, sem):
    tm = h_ref.shape[0]

    def row_copy(r, k):
        return pltpu.make_async_copy(ys_ref.at[pl.ds(dest_ref[0, 0, r * TOP_K + k], 1), :],
                                     buf.at[k, pl.ds(r, 1), :], sem)

    def issue(r, carry):
        for k in range(TOP_K):
            row_copy(r, k).start()
        return carry

    lax.fori_loop(0, tm, issue, 0)
    for k in range(TOP_K):
        pltpu.make_async_copy(ys_ref.at[pl.ds(0, tm), :], buf.at[k], sem).wait()
    gates = gate_ref[...]
    moe = gates[:, 0:1] * buf[0]
    for k in range(1, TOP_K):
        moe = moe + gates[:, k:k + 1] * buf[k]
    h = h_ref[...] + moe
    r = lax.rsqrt(jnp.mean(h * h, axis=-1, keepdims=True) + RMS_EPS)
    y_ref[...] = (h * r) * nw_ref[...]


def _combine(dest, gates, h, nw, ys, tm):
    n = h.shape[0]
    return pl.pallas_call(
        _combine_kernel,
        grid=(n // tm,),
        in_specs=[pl.BlockSpec((1, 1, tm * TOP_K), lambda i: (i, 0, 0), memory_space=pltpu.SMEM),
                  pl.BlockSpec((tm, LANES), lambda i: (i, 0)),
                  pl.BlockSpec((tm, D_MODEL), lambda i: (i, 0)),
                  pl.BlockSpec((1, D_MODEL), lambda i: (0, 0)),
                  pl.BlockSpec(memory_space=pl.ANY)],
        out_specs=pl.BlockSpec((tm, D_MODEL), lambda i: (i, 0)),
        out_shape=jax.ShapeDtypeStruct((n, D_MODEL), F32),
        scratch_shapes=[pltpu.VMEM((TOP_K, tm, D_MODEL), F32), pltpu.SemaphoreType.DMA(())],
        compiler_params=_params(("arbitrary",), 32),
        name="moe_combine",
    )(dest.reshape(n // tm, 1, tm * TOP_K), gates, h, nw, ys)


TM_TOKENS = 512
TM_EXPERT = 512
TM_COMBINE = 256
TQ_PROMPT = 256
HGRN_CHUNK = 128
HGRN_SUB = 32
PAGES_PER_STEP = 8
HGRN_SAMPLE_SEQS = 8


def _step(x_prompt, x_sample, cache_k, cache_v, cache_logf, state_hgrn, page_table,
          norm_mix_w, w_in, b_fa, lb_param, gnorm_w, w_out, norm_ffn_w,
          w_router, b_router, w_gate, b_gate, w_up, b_up, w_down, b_down, norm_final_w):
    batch, t, _ = x_prompt.shape
    dec_batch, dec_t, _ = x_sample.shape
    n_p, n_s = batch * t, dec_batch * dec_t
    n_pool = cache_k.shape[1]

    w_in_b = w_in[0].astype(BF16)
    wrec = w_in_b[:, :4 * D_REC]
    watt = w_in_b[:, 4 * D_REC:4 * D_REC + 3 * D_ATT]
    wfa = jnp.pad(w_in_b[:, 4 * D_REC + 3 * D_ATT:], ((0, 0), (0, LANES - N_HEADS)))
    wfat = jnp.pad(w_in_b[:, 4 * D_REC + 3 * D_ATT:].T, ((0, 16 - N_HEADS), (0, 0)))
    bfa = b_fa[0].reshape(1, N_HEADS)
    bfat = b_fa[0].reshape(N_HEADS, 1)
    nw_mix = norm_mix_w[0].reshape(1, D_MODEL)
    nw_ffn = norm_ffn_w[0].reshape(1, D_MODEL)
    nw_fin = norm_final_w.reshape(1, D_MODEL)
    lb = jnp.cumsum(jax.nn.softmax(lb_param.astype(F32), axis=0), axis=0)[0].reshape(1, D_REC)
    gw = gnorm_w.reshape(1, D_HEAD)
    wo = w_out[0].astype(BF16)
    wr = jnp.pad(w_router[0], ((0, 0), (0, LANES - N_EXPERTS)))
    br = jnp.pad(b_router[0], (0, LANES - N_EXPERTS), constant_values=NEG_BIG).reshape(1, LANES)
    wg, wu, wd = w_gate[0].astype(BF16), w_up[0].astype(BF16), w_down[0].astype(BF16)
    bg, bu, bd = (b[0].reshape(N_EXPERTS, 1, -1) for b in (b_gate, b_up, b_down))

    xp = x_prompt.reshape(n_p, D_MODEL)
    xs = x_sample.reshape(n_s, D_MODEL)
    tm_p, tm_s = min(TM_TOKENS, n_p), min(TM_TOKENS, n_s)
    proj = functools.partial(_inproj, nw=nw_mix, wrec=wrec, watt=watt, wfa=wfa, wfat=wfat, bfa=bfa, bfat=bfat)
    zrec_p, q_p, k_p, v_p, kb_p, vb_p, lf_p, lft_p = proj(xp, tm=tm_p)
    zrec_s, q_s, k_s, v_s, _, _, lf_s, lft_s = proj(xs, tm=tm_s)

    c_p, ct_p = _cumsum_logf(lf_p, lft_p, batch, t)
    oatt_p = _fox_prompt(q_p, kb_p, vb_p, c_p, ct_p, batch, t, TQ_PROMPT)
    orec_p, state_p = _hgrn_prompt(zrec_p, lb, gw, batch, t, HGRN_CHUNK, HGRN_SUB)

    orec_s, state_s = _hgrn_sample(zrec_s, state_hgrn[0], lb, gw, dec_batch, dec_t, HGRN_SAMPLE_SEQS)
    lft_s3 = lft_s.reshape(N_HEADS, dec_batch, dec_t).transpose(1, 0, 2)
    cache_lft = cache_logf[0].transpose(0, 2, 1)
    oatt_s = _fox_sample(page_table, q_s, k_s, v_s, lft_s3,
                         cache_k[0].reshape(n_pool, PAGE_SIZE, D_ATT), cache_v[0].reshape(n_pool, PAGE_SIZE, D_ATT),
                         cache_lft, dec_batch, dec_t, PAGES_PER_STEP)

    route = functools.partial(_outproj_router, wo=wo, nw=nw_ffn, wr=wr, br=br)
    h_p, xn_p, idx_p, gate_p, rank_p, cnt_p = route(xp, orec_p, oatt_p, cnt0=jnp.zeros((1, LANES), F32), tm=tm_p)
    h_s, xn_s, idx_s, gate_s, rank_s, cnt = route(xs, orec_s.reshape(n_s, D_REC), oatt_s.reshape(n_s, D_ATT),
                                                  cnt0=cnt_p, tm=tm_s)

    counts = cnt[0, :N_EXPERTS].astype(jnp.int32)
    padded = ((counts + TM_EXPERT - 1) // TM_EXPERT) * TM_EXPERT
    group_end = jnp.cumsum(padded)
    group_start = group_end - padded
    n_rows = (n_p + n_s) * TOP_K + N_EXPERTS * TM_EXPERT
    tile_start = jnp.arange(n_rows // TM_EXPERT, dtype=jnp.int32) * TM_EXPERT
    tile_expert = jnp.minimum(jnp.searchsorted(group_end, tile_start, side="right"), N_EXPERTS - 1).astype(jnp.int32)
    tile_valid = (tile_start < group_end[-1]).astype(jnp.int32)
    dest_p = (group_start[idx_p[:, :TOP_K]] + rank_p[:, :TOP_K]).reshape(-1)
    dest_s = (group_start[idx_s[:, :TOP_K]] + rank_s[:, :TOP_K]).reshape(-1)

    x_sorted = jnp.zeros((n_rows, D_MODEL), F32)
    x_sorted = _dispatch(dest_p, xn_p, x_sorted, tm_p)
    x_sorted = _dispatch(dest_s, xn_s, x_sorted, tm_s)
    y_sorted = _expert_mlp(tile_expert, tile_valid, x_sorted, wg, bg, wu, bu, wd, bd, TM_EXPERT)
    y_p = _combine(dest_p, gate_p, h_p, nw_fin, y_sorted, min(TM_COMBINE, n_p))
    y_s = _combine(dest_s, gate_s, h_s, nw_fin, y_sorted, min(TM_COMBINE, n_s))

    heads = lambda a, b_, t_: a.reshape(1, b_, t_, N_HEADS, D_HEAD)
    return (y_p.reshape(batch, t, D_MODEL), y_s.reshape(dec_batch, dec_t, D_MODEL),
            heads(k_p, batch, t), heads(v_p, batch, t), lf_p.reshape(1, batch, t, N_HEADS), state_p[None],
            heads(k_s, dec_batch, dec_t), heads(v_s, dec_batch, dec_t), lf_s.reshape(1, dec_batch, dec_t, N_HEADS),
            state_s[None])


def kernel(x_prompt, x_sample, cache_k, cache_v, cache_logf, state_hgrn, page_table, norm_mix_w, w_in, b_fa, lb_param, gnorm_w, w_out, norm_ffn_w, w_router, b_router, w_gate, b_gate, w_up, b_up, w_down, b_down, norm_final_w):
    return _step(x_prompt, x_sample, cache_k, cache_v, cache_logf, state_hgrn, page_table,
                 norm_mix_w, w_in, b_fa, lb_param, gnorm_w, w_out, norm_ffn_w,
                 w_router, b_router, w_gate, b_gate, w_up, b_up, w_down, b_down, norm_final_w)
```

```python
import functools

import jax
import jax.numpy as jnp
from jax import lax
from jax.experimental import pallas as pl
from jax.experimental.pallas import tpu as pltpu

D_MODEL = 1024
D_REC = 512
D_ATT = 512
N_HEADS = 8
D_HEAD = 64
N_EXPERTS = 32
TOP_K = 4
D_FF = 1024
PAGE_SIZE = 128
SWIGLU_LIMIT = 7.0
SWIGLU_ALPHA = 1.702
RMS_EPS = 1e-5

LANES = 128
NEG_BIG = -1e30
MIB = 1024 * 1024

F32 = jnp.float32
BF16 = jnp.bfloat16
HIGHEST = lax.Precision.HIGHEST

NT_DIMS = (((1,), (1,)), ((), ()))
TN_DIMS = (((0,), (0,)), ((), ()))


def _log_sigmoid(x):
    return jnp.minimum(x, 0.0) - jnp.log1p(jnp.exp(-jnp.abs(x)))


def _silu(x):
    return x * jax.nn.sigmoid(x)


def _params(sem, vmem_mib):
    return pltpu.CompilerParams(dimension_semantics=sem, vmem_limit_bytes=vmem_mib * MIB)


def _inproj_kernel(x_ref, nw_ref, wrec_ref, watt_ref, wfa_ref, wfat_ref, bfa_ref, bfat_ref,
                   zrec_ref, q_ref, k_ref, v_ref, kb_ref, vb_ref, lf_ref, lft_ref, *, kv_transposed):
    x = x_ref[...]
    r = lax.rsqrt(jnp.mean(x * x, axis=-1, keepdims=True) + RMS_EPS)
    xn = ((x * r) * nw_ref[...]).astype(BF16)
    zrec_ref[...] = jnp.dot(xn, wrec_ref[...], preferred_element_type=F32)
    za = jnp.dot(xn, watt_ref[...], preferred_element_type=F32)
    q_ref[...] = (za[:, :D_ATT] * (D_HEAD ** -0.5)).astype(BF16)
    k = za[:, D_ATT:2 * D_ATT]
    v = za[:, 2 * D_ATT:]
    if kv_transposed:
        k_ref[0] = k.T
        v_ref[0] = v.T
    else:
        k_ref[...] = k
        v_ref[...] = v
    kb_ref[...] = k.astype(BF16)
    vb_ref[...] = v.astype(BF16)
    fa = jnp.dot(xn, wfa_ref[...], preferred_element_type=F32)
    lf_ref[...] = _log_sigmoid(fa[:, :N_HEADS] + bfa_ref[...])
    fat = lax.dot_general(wfat_ref[...], xn, NT_DIMS, preferred_element_type=F32)
    lft_ref[...] = _log_sigmoid(fat[:N_HEADS, :] + bfat_ref[...])


def _inproj(x, nw, wrec, watt, wfa, wfat, bfa, bfat, tm, seq_len=None):
    n = x.shape[0]
    full = lambda a: pl.BlockSpec(a.shape, lambda i: (0,) * a.ndim)
    row = lambda w: pl.BlockSpec((tm, w), lambda i: (i, 0))
    if seq_len is None:
        kv_spec, kv_shape = row(D_ATT), jax.ShapeDtypeStruct((n, D_ATT), F32)
    else:
        nt = seq_len // tm
        kv_spec = pl.BlockSpec((1, D_ATT, tm), lambda i: (i // nt, 0, i % nt))
        kv_shape = jax.ShapeDtypeStruct((n // seq_len, D_ATT, seq_len), F32)
    return pl.pallas_call(
        functools.partial(_inproj_kernel, kv_transposed=seq_len is not None),
        grid=(n // tm,),
        in_specs=[row(D_MODEL), full(nw), full(wrec), full(watt), full(wfa), full(wfat), full(bfa), full(bfat)],
        out_specs=[row(4 * D_REC), row(D_ATT), kv_spec, kv_spec, row(D_ATT), row(D_ATT), row(N_HEADS),
                   pl.BlockSpec((N_HEADS, tm), lambda i: (0, i))],
        out_shape=[jax.ShapeDtypeStruct((n, 4 * D_REC), F32), jax.ShapeDtypeStruct((n, D_ATT), BF16),
                   kv_shape, kv_shape,
                   jax.ShapeDtypeStruct((n, D_ATT), BF16), jax.ShapeDtypeStruct((n, D_ATT), BF16),
                   jax.ShapeDtypeStruct((n, N_HEADS), F32), jax.ShapeDtypeStruct((N_HEADS, n), F32)],
        compiler_params=_params(("parallel",), 52),
        name="inproj",
    )(x, nw, wrec, watt, wfa, wfat, bfa, bfat)


def _cumsum_kernel(lf_ref, lft_ref, tri_ref, triu_ref, c_ref, ct_ref, *, tb):
    t = lf_ref.shape[1]
    tri = tri_ref[...]
    triu = triu_ref[...]
    carry = jnp.zeros((1, N_HEADS), F32)
    carry_t = jnp.zeros((N_HEADS, 1), F32)
    for j in range(t // tb):
        sl = slice(j * tb, (j + 1) * tb)
        cb = jnp.dot(tri, lf_ref[0, sl, :], precision=HIGHEST, preferred_element_type=F32) + carry
        c_ref[0, sl, :] = cb
        carry = cb[tb - 1:tb, :]
        cbt = jnp.dot(lft_ref[:, sl], triu, precision=HIGHEST, preferred_element_type=F32) + carry_t
        ct_ref[:, sl] = cbt
        carry_t = cbt[:, tb - 1:tb]


def _cumsum_logf(lf, lft, batch, t):
    tb = min(t, 512)
    idx = jnp.arange(tb)
    tri = (idx[:, None] >= idx[None, :]).astype(F32)
    return pl.pallas_call(
        functools.partial(_cumsum_kernel, tb=tb),
        grid=(batch,),
        in_specs=[pl.BlockSpec((1, t, N_HEADS), lambda b: (b, 0, 0)),
                  pl.BlockSpec((N_HEADS, t), lambda b: (0, b)),
                  pl.BlockSpec((tb, tb), lambda b: (0, 0)),
                  pl.BlockSpec((tb, tb), lambda b: (0, 0))],
        out_specs=[pl.BlockSpec((1, t, N_HEADS), lambda b: (b, 0, 0)),
                   pl.BlockSpec((N_HEADS, t), lambda b: (0, b))],
        out_shape=[jax.ShapeDtypeStruct((batch, t, N_HEADS), F32),
                   jax.ShapeDtypeStruct((N_HEADS, batch * t), F32)],
        compiler_params=_params(("parallel",), 32),
        name="cumsum_logf",
    )(lf.reshape(batch, t, N_HEADS), lft, tri, tri.T)


def _fox_prompt_kernel(q_ref, k_ref, v_ref, c_ref, ct_ref, o_ref, m_sc, acc_sc, *, tq, tk):
    p = pl.program_id(1)
    i = pl.program_id(2)
    ratio = tk // tq
    nrep = tk // LANES
    q = q_ref[...]
    lane = lax.broadcasted_iota(jnp.int32, (tq, LANES), 1)
    lane8 = lax.broadcasted_iota(jnp.int32, (tq, N_HEADS), 1)
    lane_k = lax.broadcasted_iota(jnp.int32, (tk, LANES), 1)
    cblk = c_ref[0]
    qs, cqs = [], []
    for hh in range(2):
        qs.append(jnp.where((lane // D_HEAD) == hh, q, jnp.zeros_like(q)))
        cq = jnp.sum(jnp.where(lane8 == 2 * p + hh, cblk, 0.0), axis=-1, keepdims=True)
        cqs.append(jnp.concatenate([jnp.broadcast_to(cq, (tq, LANES))] * nrep, axis=1))
    m_sc[...] = jnp.full(m_sc.shape, NEG_BIG, F32)
    acc_sc[...] = jnp.zeros(acc_sc.shape, F32)

    def step(j, masked):
        off = pl.multiple_of(j * tk, tk)
        kb = k_ref[pl.ds(off, tk), :]
        vb = v_ref[pl.ds(off, tk), :]
        for hh in range(2):
            va = jnp.where((lane_k // D_HEAD) == hh, vb, jnp.ones_like(vb))
            s = lax.dot_general(qs[hh], kb, NT_DIMS, preferred_element_type=F32)
            s = (s + cqs[hh]) - ct_ref[0, hh:hh + 1, pl.ds(off, tk)]
            if masked:
                row = lax.broadcasted_iota(jnp.int32, (tq, tk), 0) + (i % ratio) * tq
                col = lax.broadcasted_iota(jnp.int32, (tq, tk), 1)
                s = jnp.where(col <= row, s, NEG_BIG)
            m_prev = m_sc[hh]
            m_new = jnp.maximum(m_prev, jnp.max(s, axis=-1, keepdims=True))
            alpha = jnp.exp(m_prev - m_new)
            pr = jnp.exp(s - jnp.concatenate([m_new] * nrep, axis=1)).astype(BF16)
            acc_sc[hh] = alpha * acc_sc[hh] + jnp.dot(pr, va, preferred_element_type=F32)
            m_sc[hh] = m_new

    def body(j, carry):
        step(j, False)
        return carry

    lax.fori_loop(0, i // ratio, body, 0)
    step(i // ratio, True)
    outs = []
    for hh in range(2):
        a = acc_sc[hh]
        outs.append(a / pltpu.roll(a, D_HEAD, axis=1))
    o_ref[...] = jnp.where(lane < D_HEAD, outs[0], outs[1]).astype(o_ref.dtype)


def _fox_prompt(q, kb, vb, c, ct, batch, t, tq, tk):
    nq = t // tq
    ct4 = ct.reshape(N_HEADS // 2, 2, batch * t)
    return pl.pallas_call(
        functools.partial(_fox_prompt_kernel, tq=tq, tk=tk),
        grid=(batch, N_HEADS // 2, nq),
        in_specs=[pl.BlockSpec((tq, LANES), lambda b, p, i: (b * nq + i, p)),
                  pl.BlockSpec((t, LANES), lambda b, p, i: (b, p)),
                  pl.BlockSpec((t, LANES), lambda b, p, i: (b, p)),
                  pl.BlockSpec((1, tq, N_HEADS), lambda b, p, i: (b, i, 0)),
                  pl.BlockSpec((1, 2, t), lambda b, p, i: (p, 0, b))],
        out_specs=pl.BlockSpec((tq, LANES), lambda b, p, i: (b * nq + i, p)),
        out_shape=jax.ShapeDtypeStruct((batch * t, D_ATT), BF16),
        scratch_shapes=[pltpu.VMEM((2, tq, LANES), F32), pltpu.VMEM((2, tq, LANES), F32)],
        compiler_params=_params(("parallel", "parallel", "parallel"), 32),
        name="fox_prompt",
    )(q, kb, vb, c, ct4)


def _hgrn_chunk(zq, zf, zi, lb, gw, tri, s_sc, o_sc, *, chunk, sub, n_valid):
    fg = lb + (1.0 - lb) * jax.nn.sigmoid(zf)
    kk = 1.0 - fg
    g = jnp.log(fg)
    if n_valid < chunk:
        valid = lax.broadcasted_iota(jnp.int32, (chunk, D_REC), 0) < n_valid
        kk = jnp.where(valid, kk, 0.0)
        g = jnp.where(valid, g, 0.0)
    q = _silu(zq)
    b = jnp.dot(tri, g, precision=HIGHEST, preferred_element_type=F32)
    b_end = b[chunk - 1:chunk, :]
    qhat = (q * jnp.exp(b)).astype(BF16)
    khat = (kk * jnp.exp(b_end - b)).astype(BF16)
    vb = zi.astype(BF16)
    e_end = jnp.exp(b_end)
    s_old = [s_sc[h] for h in range(N_HEADS)]
    s_old_b = [s.astype(BF16) for s in s_old]
    for i in range(chunk // sub):
        r0, r1 = i * sub, (i + 1) * sub
        rho = b[r0 + sub // 2:r0 + sub // 2 + 1, :]
        qt = (q[r0:r1] * jnp.exp(b[r0:r1] - rho)).astype(BF16)
        kt = (kk[:r1] * jnp.exp(rho - b[:r1])).astype(BF16)
        row = lax.broadcasted_iota(jnp.int32, (sub, r1), 0) + r0
        col = lax.broadcasted_iota(jnp.int32, (sub, r1), 1)
        for h in range(N_HEADS):
            hs = slice(h * D_HEAD, (h + 1) * D_HEAD)
            a = lax.dot_general(qt[:, hs], kt[:, hs], NT_DIMS, preferred_element_type=F32)
            a = jnp.where(col <= row, a, 0.0).astype(BF16)
            o = (jnp.dot(a, vb[:r1, hs], preferred_element_type=F32)
                 + jnp.dot(qhat[r0:r1, hs], s_old_b[h], preferred_element_type=F32))
            ms = jnp.mean(o * o, axis=-1, keepdims=True)
            o_sc[r0:r1, hs] = (o * lax.rsqrt(ms + RMS_EPS)) * gw
    eye = (lax.broadcasted_iota(jnp.int32, (D_HEAD, D_HEAD), 0)
           == lax.broadcasted_iota(jnp.int32, (D_HEAD, D_HEAD), 1))
    for h in range(N_HEADS):
        hs = slice(h * D_HEAD, (h + 1) * D_HEAD)
        e_col = jnp.sum(jnp.where(eye, jnp.broadcast_to(e_end[:, hs], (D_HEAD, D_HEAD)), 0.0), axis=1, keepdims=True)
        s_sc[h] = e_col * s_old[h] + lax.dot_general(khat[:, hs], vb[:, hs], TN_DIMS, preferred_element_type=F32)


def _hgrn_prompt_kernel(z_ref, lb_ref, gw_ref, tri_ref, o_ref, s_ref, s_sc, o_sc, *, chunk, sub):
    j = pl.program_id(1)

    @pl.when(j == 0)
    def _():
        s_sc[...] = jnp.zeros(s_sc.shape, F32)

    z = z_ref[...]
    _hgrn_chunk(z[:, :D_REC], z[:, D_REC:2 * D_REC], z[:, 2 * D_REC:3 * D_REC], lb_ref[...], gw_ref[...],
                tri_ref[...], s_sc, o_sc, chunk=chunk, sub=sub, n_valid=chunk)
    o_ref[...] = (o_sc[...] * _silu(z[:, 3 * D_REC:])).astype(o_ref.dtype)

    @pl.when(j == pl.num_programs(1) - 1)
    def _():
        s_ref[0] = s_sc[...]


def _hgrn_prompt(zrec, lb, gw, batch, t, chunk, sub):
    nc = t // chunk
    idx = jnp.arange(chunk)
    tri = (idx[:, None] >= idx[None, :]).astype(F32)
    return pl.pallas_call(
        functools.partial(_hgrn_prompt_kernel, chunk=chunk, sub=sub),
        grid=(batch, nc),
        in_specs=[pl.BlockSpec((chunk, 4 * D_REC), lambda b, j: (b * nc + j, 0)),
                  pl.BlockSpec((1, D_REC), lambda b, j: (0, 0)),
                  pl.BlockSpec((1, D_HEAD), lambda b, j: (0, 0)),
                  pl.BlockSpec((chunk, chunk), lambda b, j: (0, 0))],
        out_specs=[pl.BlockSpec((chunk, D_REC), lambda b, j: (b * nc + j, 0)),
                   pl.BlockSpec((1, N_HEADS, D_HEAD, D_HEAD), lambda b, j: (b, 0, 0, 0))],
        out_shape=[jax.ShapeDtypeStruct((batch * t, D_REC), BF16),
                   jax.ShapeDtypeStruct((batch, N_HEADS, D_HEAD, D_HEAD), F32)],
        scratch_shapes=[pltpu.VMEM((N_HEADS, D_HEAD, D_HEAD), F32), pltpu.VMEM((chunk, D_REC), F32)],
        compiler_params=_params(("parallel", "arbitrary"), 32),
        name="hgrn_prompt",
    )(zrec, lb, gw, tri)


def _hgrn_sample_kernel(z_ref, s0_ref, lb_ref, gw_ref, tri_ref, o_ref, s_ref, s_sc, o_sc, *, seqs, t, chunk):
    lb = lb_ref[...]
    gw = gw_ref[...]
    tri = tri_ref[...]
    pad = jnp.zeros((chunk - t, D_REC), F32)
    for s in range(seqs):
        z = z_ref[s]
        s_sc[...] = s0_ref[s]
        zq, zf, zi = (jnp.concatenate([z[:, u * D_REC:(u + 1) * D_REC], pad], axis=0) for u in range(3))
        _hgrn_chunk(zq, zf, zi, lb, gw, tri, s_sc, o_sc, chunk=chunk, sub=chunk, n_valid=t)
        o_ref[s] = (o_sc[:t, :] * _silu(z[:, 3 * D_REC:])).astype(o_ref.dtype)
        s_ref[s] = s_sc[...]


def _hgrn_sample(zrec, s0, lb, gw, batch, t, seqs):
    chunk = 8
    idx = jnp.arange(chunk)
    tri = (idx[:, None] >= idx[None, :]).astype(F32)
    return pl.pallas_call(
        functools.partial(_hgrn_sample_kernel, seqs=seqs, t=t, chunk=chunk),
        grid=(batch // seqs,),
        in_specs=[pl.BlockSpec((seqs, t, 4 * D_REC), lambda b: (b, 0, 0)),
                  pl.BlockSpec((seqs, N_HEADS, D_HEAD, D_HEAD), lambda b: (b, 0, 0, 0)),
                  pl.BlockSpec((1, D_REC), lambda b: (0, 0)),
                  pl.BlockSpec((1, D_HEAD), lambda b: (0, 0)),
                  pl.BlockSpec((chunk, chunk), lambda b: (0, 0))],
        out_specs=[pl.BlockSpec((seqs, t, D_REC), lambda b: (b, 0, 0)),
                   pl.BlockSpec((seqs, N_HEADS, D_HEAD, D_HEAD), lambda b: (b, 0, 0, 0))],
        out_shape=[jax.ShapeDtypeStruct((batch, t, D_REC), BF16),
                   jax.ShapeDtypeStruct((batch, N_HEADS, D_HEAD, D_HEAD), F32)],
        scratch_shapes=[pltpu.VMEM((N_HEADS, D_HEAD, D_HEAD), F32), pltpu.VMEM((chunk, D_REC), F32)],
        compiler_params=_params(("parallel",), 32),
        name="hgrn_sample",
    )(zrec.reshape(batch, t, 4 * D_REC), s0, lb, gw, tri)


def _fox_sample_kernel(pt_ref, q_ref, kn_ref, vn_ref, lft_ref, gs_ref, *rest, t, ppb):
    k_refs = rest[:ppb]
    v_refs = rest[ppb:2 * ppb]
    lf_refs = rest[2 * ppb:3 * ppb]
    o_ref = rest[3 * ppb]
    m_sc, l_sc, acc_sc, carry_sc, cq_sc = rest[3 * ppb + 1:]
    del pt_ref
    j = pl.program_id(1)
    nrow = t * N_HEADS
    lane = lax.broadcasted_iota(jnp.int32, (nrow, D_ATT), 1)
    rowi = lax.broadcasted_iota(jnp.int32, (nrow, D_ATT), 0)
    head_lanes = (lane // D_HEAD) == (rowi % N_HEADS)
    q = q_ref[0]
    qbd = jnp.concatenate([jnp.broadcast_to(q[u:u + 1, :], (N_HEADS, D_ATT)) for u in range(t)], axis=0)
    qbd = jnp.where(head_lanes, qbd, jnp.zeros_like(qbd))

    @pl.when(j == 0)
    def _():
        lft = lft_ref[0]
        cols = [lft[:, 0:1]]
        for u in range(1, t):
            cols.append(cols[-1] + lft[:, u:u + 1])
        cq_t = jnp.concatenate(cols, axis=1)
        cq_col = jnp.concatenate(cols, axis=0)
        cq_sc[...] = cq_col
        zpad = jnp.zeros((N_HEADS - t, D_ATT), F32)
        kn = jnp.concatenate([kn_ref[0], zpad], axis=0).astype(BF16)
        vn = jnp.concatenate([vn_ref[0], zpad], axis=0).astype(BF16)
        cq_t8 = jnp.concatenate([cq_t, jnp.zeros((N_HEADS, N_HEADS - t), F32)], axis=1)
        s = lax.dot_general(qbd, kn, NT_DIMS, preferred_element_type=F32)
        s = s + cq_col - jnp.concatenate([cq_t8] * t, axis=0)
        r2 = lax.broadcasted_iota(jnp.int32, (nrow, N_HEADS), 0) // N_HEADS
        c2 = lax.broadcasted_iota(jnp.int32, (nrow, N_HEADS), 1)
        s = jnp.where(c2 <= r2, s, NEG_BIG)
        m0 = jnp.max(s, axis=-1, keepdims=True)
        p0 = jnp.exp(s - m0)
        m_sc[...] = m0
        l_sc[...] = jnp.sum(p0, axis=-1, keepdims=True)
        acc_sc[...] = jnp.dot(p0.astype(BF16), vn, preferred_element_type=F32)
        carry_sc[...] = jnp.zeros(carry_sc.shape, F32)

    lf_all = jnp.concatenate([r[0] for r in lf_refs], axis=0)
    insuf = jnp.dot(lf_all, gs_ref[...], precision=HIGHEST, preferred_element_type=F32)
    carry = carry_sc[...]
    biases = []
    for u in range(ppb):
        biases.append(insuf[u * N_HEADS:(u + 1) * N_HEADS, :] + carry)
        carry = carry + jnp.sum(lf_refs[u][0], axis=-1, keepdims=True)
    carry_sc[...] = carry
    bias = jnp.concatenate(biases, axis=1)
    bias = jnp.concatenate([bias] * t, axis=0) + cq_sc[...]
    kcat = jnp.concatenate([r[0].astype(BF16) for r in k_refs], axis=1)
    vcat = jnp.concatenate([r[0].astype(BF16) for r in v_refs], axis=1)
    s = jnp.dot(qbd, kcat, preferred_element_type=F32) + bias
    m_prev = m_sc[...]
    m_new = jnp.maximum(m_prev, jnp.max(s, axis=-1, keepdims=True))
    alpha = jnp.exp(m_prev - m_new)
    pr = jnp.exp(s - m_new)
    l_sc[...] = alpha * l_sc[...] + jnp.sum(pr, axis=-1, keepdims=True)
    acc_sc[...] = alpha * acc_sc[...] + lax.dot_general(pr.astype(BF16), vcat, NT_DIMS, preferred_element_type=F32)
    m_sc[...] = m_new

    @pl.when(j == pl.num_programs(1) - 1)
    def _():
        o = jnp.where(head_lanes, acc_sc[...] / l_sc[...], 0.0)
        o_ref[0] = jnp.sum(o.reshape(t, N_HEADS, D_ATT), axis=1).astype(o_ref.dtype)


def _fox_sample(page_table, q, kn, vn, lft, cache_k, cache_v, cache_lft, batch, t, ppb):
    n_pages = page_table.shape[1]
    nsteps = n_pages // ppb
    idx = jnp.arange(PAGE_SIZE)
    gs = (idx[:, None] > idx[None, :]).astype(F32)
    seq = lambda w: pl.BlockSpec((1, t, w), lambda b, j, pt: (b, 0, 0))

    def page(u, shape):
        return pl.BlockSpec((1,) + shape, lambda b, j, pt: (pt[b, n_pages - 1 - (j * ppb + u)], 0, 0))

    in_specs = ([seq(D_ATT), seq(D_ATT), seq(D_ATT),
                 pl.BlockSpec((1, N_HEADS, t), lambda b, j, pt: (b, 0, 0)),
                 pl.BlockSpec((PAGE_SIZE, PAGE_SIZE), lambda b, j, pt: (0, 0))]
                + [page(u, (D_ATT, PAGE_SIZE)) for u in range(ppb)]
                + [page(u, (D_ATT, PAGE_SIZE)) for u in range(ppb)]
                + [page(u, (N_HEADS, PAGE_SIZE)) for u in range(ppb)])
    nrow = t * N_HEADS
    grid_spec = pltpu.PrefetchScalarGridSpec(
        num_scalar_prefetch=1,
        grid=(batch, nsteps),
        in_specs=in_specs,
        out_specs=pl.BlockSpec((1, t, D_ATT), lambda b, j, pt: (b, 0, 0)),
        scratch_shapes=[pltpu.VMEM((nrow, 1), F32), pltpu.VMEM((nrow, 1), F32), pltpu.VMEM((nrow, D_ATT), F32),
                        pltpu.VMEM((N_HEADS, 1), F32), pltpu.VMEM((nrow, 1), F32)],
    )
    return pl.pallas_call(
        functools.partial(_fox_sample_kernel, t=t, ppb=ppb),
        grid_spec=grid_spec,
        out_shape=jax.ShapeDtypeStruct((batch, t, D_ATT), BF16),
        compiler_params=_params(("parallel", "arbitrary"), 40),
        name="fox_sample",
    )(page_table, q.reshape(batch, t, D_ATT), kn.reshape(batch, t, D_ATT), vn.reshape(batch, t, D_ATT), lft, gs,
      *([cache_k] * ppb), *([cache_v] * ppb), *([cache_lft] * ppb))


def _outproj_router_kernel(x_ref, orec_ref, oatt_ref, wo_ref, nw_ref, wr_ref, br_ref, cnt0_ref, tril_ref,
                           h_ref, xn_ref, idx_ref, gate_ref, rank_ref, cnt_ref, base_sc):
    i = pl.program_id(0)

    @pl.when(i == 0)
    def _():
        base_sc[...] = cnt0_ref[...]

    tm = x_ref.shape[0]
    h = (x_ref[...]
         + jnp.dot(orec_ref[...], wo_ref[:D_REC, :], preferred_element_type=F32)
         + jnp.dot(oatt_ref[...], wo_ref[D_REC:, :], preferred_element_type=F32))
    h_ref[...] = h
    r = lax.rsqrt(jnp.mean(h * h, axis=-1, keepdims=True) + RMS_EPS)
    xn = (h * r) * nw_ref[...]
    xn_ref[...] = xn
    logits = jnp.dot(xn, wr_ref[...], precision=HIGHEST, preferred_element_type=F32) + br_ref[...]
    lane = lax.broadcasted_iota(jnp.int32, (tm, LANES), 1)
    vals, hots = [], []
    idx_out = jnp.zeros((tm, LANES), jnp.int32)
    for k in range(TOP_K):
        m = jnp.max(logits, axis=-1, keepdims=True)
        sel = jnp.min(jnp.where(logits == m, lane, LANES), axis=-1, keepdims=True)
        hot = lane == sel
        vals.append(m)
        hots.append(hot)
        idx_out = jnp.where(lane == k, sel, idx_out)
        logits = jnp.where(hot, -jnp.inf, logits)
    idx_ref[...] = idx_out
    es = [jnp.exp(v - vals[0]) for v in vals]
    denom = es[0] + es[1] + es[2] + es[3]
    gate_out = jnp.zeros((tm, LANES), F32)
    for k in range(TOP_K):
        gate_out = jnp.where(lane == k, es[k] / denom, gate_out)
    gate_ref[...] = gate_out
    base = base_sc[...]
    rank_out = jnp.zeros((tm, LANES), jnp.int32)
    for k in range(TOP_K):
        hot_f = hots[k].astype(F32)
        before = jnp.dot(tril_ref[...], hot_f.astype(BF16), preferred_element_type=F32) + base
        rank = jnp.sum(jnp.where(hots[k], before, 0.0), axis=-1, keepdims=True)
        rank_out = jnp.where(lane == k, rank.astype(jnp.int32), rank_out)
        base = base + jnp.sum(hot_f, axis=0, keepdims=True)
    rank_ref[...] = rank_out
    base_sc[...] = base
    cnt_ref[...] = base


def _outproj_router(x, orec, oatt, wo, nw, wr, br, cnt0, tm):
    n = x.shape[0]
    idx = jnp.arange(tm)
    tril = (idx[:, None] > idx[None, :]).astype(BF16)
    full = lambda a: pl.BlockSpec(a.shape, lambda i: (0,) * a.ndim)
    row = lambda w: pl.BlockSpec((tm, w), lambda i: (i, 0))
    return pl.pallas_call(
        _outproj_router_kernel,
        grid=(n // tm,),
        in_specs=[row(D_MODEL), row(D_REC), row(D_ATT), full(wo), full(nw), full(wr), full(br), full(cnt0), full(tril)],
        out_specs=[row(D_MODEL), row(D_MODEL), row(LANES), row(LANES), row(LANES),
                   pl.BlockSpec((1, LANES), lambda i: (0, 0))],
        out_shape=[jax.ShapeDtypeStruct((n, D_MODEL), F32), jax.ShapeDtypeStruct((n, D_MODEL), F32),
                   jax.ShapeDtypeStruct((n, LANES), jnp.int32), jax.ShapeDtypeStruct((n, LANES), F32),
                   jax.ShapeDtypeStruct((n, LANES), jnp.int32), jax.ShapeDtypeStruct((1, LANES), F32)],
        scratch_shapes=[pltpu.VMEM((1, LANES), F32)],
        compiler_params=_params(("arbitrary",), 40),
        name="outproj_router",
    )(x, orec, oatt, wo, nw, wr, br, cnt0, tril)


def _dispatch_kernel(gend_ref, gsize_ref, total_ref, dest_ref, xp_ref, xs_ref, out_ref, zbuf, sem, zsem,
                     *, n_prompt_tiles, tile_rows):
    i = pl.program_id(0)
    n_tiles = out_ref.shape[0] // tile_rows

    @pl.when(i == 0)
    def _():
        zbuf[...] = jnp.zeros(zbuf.shape, zbuf.dtype)

        def zero_tile(start):
            cp = pltpu.make_async_copy(zbuf, out_ref.at[pl.ds(pl.multiple_of(start, tile_rows), tile_rows), :], zsem)
            cp.start()
            cp.wait()

        for e in range(N_EXPERTS):
            @pl.when(gsize_ref[e] > 0)
            def _():
                zero_tile(gend_ref[e] - tile_rows)

        def tail(tt, carry):
            zero_tile(tt * tile_rows)
            return carry

        lax.fori_loop(total_ref[0] // tile_rows, n_tiles, tail, 0)

    def scatter(src_ref):
        n = src_ref.shape[0]

        def issue(r, carry):
            for k in range(TOP_K):
                pltpu.make_async_copy(src_ref.at[pl.ds(r, 1), :],
                                      out_ref.at[pl.ds(dest_ref[0, 0, r * TOP_K + k], 1), :],
                                      sem).start(priority=k % 2)
            return carry

        lax.fori_loop(0, n, issue, 0)
        for k in range(TOP_K):
            pltpu.make_async_copy(src_ref, out_ref.at[pl.ds(0, n), :], sem).wait()

    @pl.when(i < n_prompt_tiles)
    def _():
        scatter(xp_ref)

    @pl.when(i == n_prompt_tiles)
    def _():
        scatter(xs_ref)


def _dispatch(group_end, group_size, dest_p, dest_s, xn_p, xn_s, n_rows, tm, tile_rows):
    n_p, n_s = xn_p.shape[0], xn_s.shape[0]
    npt = n_p // tm
    dest = jnp.concatenate([dest_p.reshape(npt, tm * TOP_K),
                            jnp.pad(dest_s, (0, (tm - n_s) * TOP_K)).reshape(1, tm * TOP_K)], axis=0)
    grid_spec = pltpu.PrefetchScalarGridSpec(
        num_scalar_prefetch=3,
        grid=(npt + 1,),
        in_specs=[pl.BlockSpec((1, 1, tm * TOP_K), lambda i, *_: (i, 0, 0), memory_space=pltpu.SMEM),
                  pl.BlockSpec((tm, D_MODEL), lambda i, *_: (jnp.minimum(i, npt - 1), 0)),
                  pl.BlockSpec((n_s, D_MODEL), lambda i, *_: (0, 0))],
        out_specs=pl.BlockSpec(memory_space=pl.ANY),
        scratch_shapes=[pltpu.VMEM((tile_rows, D_MODEL), F32), pltpu.SemaphoreType.DMA(()),
                        pltpu.SemaphoreType.DMA(())],
    )
    return pl.pallas_call(
        functools.partial(_dispatch_kernel, n_prompt_tiles=npt, tile_rows=tile_rows),
        grid_spec=grid_spec,
        out_shape=jax.ShapeDtypeStruct((n_rows, D_MODEL), F32),
        compiler_params=_params(("arbitrary",), 32),
        name="moe_dispatch",
    )(group_end, group_size, group_end[-1:], dest.reshape(npt + 1, 1, tm * TOP_K), xn_p, xn_s)


def _expert_mlp_kernel(te_ref, tv_ref, tf_ref, x_ref, wg_ref, bg_ref, wu_ref, bu_ref, wd_ref, bd_ref, y_ref,
                       wg_sc, wu_sc, wd_sc):
    i = pl.program_id(0)
    del te_ref

    @pl.when(tf_ref[i] != 0)
    def _():
        wg_sc[...] = wg_ref[0].astype(BF16)
        wu_sc[...] = wu_ref[0].astype(BF16)
        wd_sc[...] = wd_ref[0].astype(BF16)

    @pl.when(tv_ref[i] != 0)
    def _():
        x = x_ref[...].astype(BF16)
        gate = jnp.minimum(jnp.dot(x, wg_sc[...], preferred_element_type=F32) + bg_ref[0], SWIGLU_LIMIT)
        up = jnp.clip(jnp.dot(x, wu_sc[...], preferred_element_type=F32) + bu_ref[0], -SWIGLU_LIMIT, SWIGLU_LIMIT)
        glu = gate * jax.nn.sigmoid(gate * SWIGLU_ALPHA)
        mid = ((up + 1.0) * glu).astype(BF16)
        y_ref[...] = jnp.dot(mid, wd_sc[...], preferred_element_type=F32) + bd_ref[0]

    @pl.when(tv_ref[i] == 0)
    def _():
        y_ref[...] = jnp.zeros(y_ref.shape, y_ref.dtype)


def _expert_mlp(tile_expert, tile_valid, tile_first, xs, wg, bg, wu, bu, wd, bd, tm):
    p = xs.shape[0]
    wspec = lambda: pl.BlockSpec((1, D_MODEL, D_FF), lambda i, te, tv, tf: (te[i], 0, 0))
    bspec = lambda: pl.BlockSpec((1, 1, D_FF), lambda i, te, tv, tf: (te[i], 0, 0))
    grid_spec = pltpu.PrefetchScalarGridSpec(
        num_scalar_prefetch=3,
        grid=(p // tm,),
        in_specs=[pl.BlockSpec((tm, D_MODEL), lambda i, te, tv, tf: (i * tv[i], 0)),
                  wspec(), bspec(), wspec(), bspec(), wspec(), bspec()],
        out_specs=pl.BlockSpec((tm, D_MODEL), lambda i, te, tv, tf: (i, 0)),
        scratch_shapes=[pltpu.VMEM((D_MODEL, D_FF), BF16), pltpu.VMEM((D_MODEL, D_FF), BF16),
                        pltpu.VMEM((D_FF, D_MODEL), BF16)],
    )
    return pl.pallas_call(
        _expert_mlp_kernel,
        grid_spec=grid_spec,
        out_shape=jax.ShapeDtypeStruct((p, D_MODEL), F32),
        compiler_params=_params(("arbitrary",), 56),
        name="expert_mlp",
    )(tile_expert, tile_valid, tile_first, xs, wg, bg, wu, bu, wd, bd)


def _combine_kernel(dest_ref, gate_ref, h_ref, nw_ref, ys_ref, y_ref, buf, sem):
    tm = h_ref.shape[0]

    def row_copy(r, k):
        return pltpu.make_async_copy(ys_ref.at[pl.ds(dest_ref[0, 0, r * TOP_K + k], 1), :],
                                     buf.at[k, pl.ds(r, 1), :], sem)

    def issue(r, carry):
        for k in range(TOP_K):
            row_copy(r, k).start(priority=k % 2)
        return carry

    lax.fori_loop(0, tm, issue, 0)
    for k in range(TOP_K):
        pltpu.make_async_copy(ys_ref.at[pl.ds(0, tm), :], buf.at[k], sem).wait()
    gates = gate_ref[...]
    moe = gates[:, 0:1] * buf[0]
    for k in range(1, TOP_K):
        moe = moe + gates[:, k:k + 1] * buf[k]
    h = h_ref[...] + moe
    r = lax.rsqrt(jnp.mean(h * h, axis=-1, keepdims=True) + RMS_EPS)
    y_ref[...] = (h * r) * nw_ref[...]


def _combine(dest, gates, h, nw, ys, tm):
    n = h.shape[0]
    return pl.pallas_call(
        _combine_kernel,
        grid=(n // tm,),
        in_specs=[pl.BlockSpec((1, 1, tm * TOP_K), lambda i: (i, 0, 0), memory_space=pltpu.SMEM),
                  pl.BlockSpec((tm, LANES), lambda i: (i, 0)),
                  pl.BlockSpec((tm, D_MODEL), lambda i: (i, 0)),
                  pl.BlockSpec((1, D_MODEL), lambda i: (0, 0)),
                  pl.BlockSpec(memory_space=pl.ANY)],
        out_specs=pl.BlockSpec((tm, D_MODEL), lambda i: (i, 0)),
        out_shape=jax.ShapeDtypeStruct((n, D_MODEL), F32),
        scratch_shapes=[pltpu.VMEM((TOP_K, tm, D_MODEL), F32), pltpu.SemaphoreType.DMA(())],
        compiler_params=_params(("arbitrary",), 32),
        name="moe_combine",
    )(dest.reshape(n // tm, 1, tm * TOP_K), gates, h, nw, ys)


TM_TOKENS = 512
TM_EXPERT = 512
TM_COMBINE = 256
TQ_PROMPT = 512
TK_PROMPT = 512
HGRN_CHUNK = 128
HGRN_SUB = 32
PAGES_PER_STEP = 8
HGRN_SAMPLE_SEQS = 8


def _step(x_prompt, x_sample, cache_k, cache_v, cache_logf, state_hgrn, page_table,
          norm_mix_w, w_in, b_fa, lb_param, gnorm_w, w_out, norm_ffn_w,
          w_router, b_router, w_gate, b_gate, w_up, b_up, w_down, b_down, norm_final_w):
    batch, t, _ = x_prompt.shape
    dec_batch, dec_t, _ = x_sample.shape
    n_p, n_s = batch * t, dec_batch * dec_t
    n_pool = cache_k.shape[1]

    w_in_b = w_in[0].astype(BF16)
    wrec = w_in_b[:, :4 * D_REC]
    watt = w_in_b[:, 4 * D_REC:4 * D_REC + 3 * D_ATT]
    wfa = jnp.pad(w_in_b[:, 4 * D_REC + 3 * D_ATT:], ((0, 0), (0, LANES - N_HEADS)))
    wfat = jnp.pad(w_in_b[:, 4 * D_REC + 3 * D_ATT:].T, ((0, 16 - N_HEADS), (0, 0)))
    bfa = b_fa[0].reshape(1, N_HEADS)
    bfat = b_fa[0].reshape(N_HEADS, 1)
    nw_mix = norm_mix_w[0].reshape(1, D_MODEL)
    nw_ffn = norm_ffn_w[0].reshape(1, D_MODEL)
    nw_fin = norm_final_w.reshape(1, D_MODEL)
    lb = jnp.cumsum(jax.nn.softmax(lb_param.astype(F32), axis=0), axis=0)[0].reshape(1, D_REC)
    gw = gnorm_w.reshape(1, D_HEAD)
    wo = w_out[0].astype(BF16)
    wr = jnp.pad(w_router[0], ((0, 0), (0, LANES - N_EXPERTS)))
    br = jnp.pad(b_router[0], (0, LANES - N_EXPERTS), constant_values=NEG_BIG).reshape(1, LANES)
    wg, wu, wd = w_gate[0], w_up[0], w_down[0]
    bg, bu, bd = (b[0].reshape(N_EXPERTS, 1, -1) for b in (b_gate, b_up, b_down))

    xp = x_prompt.reshape(n_p, D_MODEL)
    xs = x_sample.reshape(n_s, D_MODEL)
    tm_p, tm_s = min(TM_TOKENS, n_p), min(TM_TOKENS, n_s)
    proj = functools.partial(_inproj, nw=nw_mix, wrec=wrec, watt=watt, wfa=wfa, wfat=wfat, bfa=bfa, bfat=bfat)
    zrec_p, q_p, kt_p, vt_p, kb_p, vb_p, lf_p, lft_p = proj(xp, tm=tm_p, seq_len=t)
    zrec_s, q_s, k_s, v_s, _, _, lf_s, lft_s = proj(xs, tm=tm_s)

    c_p, ct_p = _cumsum_logf(lf_p, lft_p, batch, t)
    oatt_p = _fox_prompt(q_p, kb_p, vb_p, c_p, ct_p, batch, t, TQ_PROMPT, TK_PROMPT)
    orec_p, state_p = _hgrn_prompt(zrec_p, lb, gw, batch, t, HGRN_CHUNK, HGRN_SUB)

    orec_s, state_s = _hgrn_sample(zrec_s, state_hgrn[0], lb, gw, dec_batch, dec_t, HGRN_SAMPLE_SEQS)
    lft_s3 = lft_s.reshape(N_HEADS, dec_batch, dec_t).transpose(1, 0, 2)
    cache_lft = cache_logf[0].transpose(0, 2, 1)
    oatt_s = _fox_sample(page_table, q_s, k_s, v_s, lft_s3,
                         cache_k[0].transpose(0, 2, 3, 1).reshape(n_pool, D_ATT, PAGE_SIZE),
                         cache_v[0].transpose(0, 2, 3, 1).reshape(n_pool, D_ATT, PAGE_SIZE),
                         cache_lft, dec_batch, dec_t, PAGES_PER_STEP)

    route = functools.partial(_outproj_router, wo=wo, nw=nw_ffn, wr=wr, br=br)
    h_p, xn_p, idx_p, gate_p, rank_p, cnt_p = route(xp, orec_p, oatt_p, cnt0=jnp.zeros((1, LANES), F32), tm=tm_p)
    h_s, xn_s, idx_s, gate_s, rank_s, cnt = route(xs, orec_s.reshape(n_s, D_REC), oatt_s.reshape(n_s, D_ATT),
                                                  cnt0=cnt_p, tm=tm_s)

    counts = cnt[0, :N_EXPERTS].astype(jnp.int32)
    padded = ((counts + TM_EXPERT - 1) // TM_EXPERT) * TM_EXPERT
    group_end = jnp.cumsum(padded)
    group_start = group_end - padded
    n_rows = (n_p + n_s) * TOP_K + N_EXPERTS * TM_EXPERT
    tile_start = jnp.arange(n_rows // TM_EXPERT, dtype=jnp.int32) * TM_EXPERT
    tile_expert = jnp.minimum(jnp.sum((tile_start[:, None] >= group_end[None, :]).astype(jnp.int32), axis=1),
                              N_EXPERTS - 1)
    tile_valid = (tile_start < group_end[-1]).astype(jnp.int32)
    tile_first = jnp.concatenate([jnp.ones((1,), jnp.int32),
                                  (tile_expert[1:] != tile_expert[:-1]).astype(jnp.int32)])
    dest_p = (group_start[idx_p[:, :TOP_K]] + rank_p[:, :TOP_K]).reshape(-1)
    dest_s = (group_start[idx_s[:, :TOP_K]] + rank_s[:, :TOP_K]).reshape(-1)

    x_sorted = _dispatch(group_end.astype(jnp.int32), padded, dest_p, dest_s, xn_p, xn_s, n_rows, tm_p, TM_EXPERT)
    y_sorted = _expert_mlp(tile_expert, tile_valid, tile_first, x_sorted, wg, bg, wu, bu, wd, bd, TM_EXPERT)
    y_p = _combine(dest_p, gate_p, h_p, nw_fin, y_sorted, min(TM_COMBINE, n_p))
    y_s = _combine(dest_s, gate_s, h_s, nw_fin, y_sorted, min(TM_COMBINE, n_s))

    heads = lambda a, b_, t_: a.reshape(1, b_, t_, N_HEADS, D_HEAD)
    heads_t = lambda a: a.reshape(1, batch, N_HEADS, D_HEAD, t).transpose(0, 1, 4, 2, 3)
    return (y_p.reshape(batch, t, D_MODEL), y_s.reshape(dec_batch, dec_t, D_MODEL),
            heads_t(kt_p), heads_t(vt_p), lf_p.reshape(1, batch, t, N_HEADS), state_p[None],
            heads(k_s, dec_batch, dec_t), heads(v_s, dec_batch, dec_t), lf_s.reshape(1, dec_batch, dec_t, N_HEADS),
            state_s[None])


def kernel(x_prompt, x_sample, cache_k, cache_v, cache_logf, state_hgrn, page_table, norm_mix_w, w_in, b_fa, lb_param, gnorm_w, w_out, norm_ffn_w, w_router, b_router, w_gate, b_gate, w_up, b_up, w_down, b_down, norm_final_w):
    return _step(x_prompt, x_sample, cache_k, cache_v, cache_logf, state_hgrn, page_table,
                 norm_mix_w, w_in, b_fa, lb_param, gnorm_w, w_out, norm_ffn_w,
                 w_router, b_router, w_gate, b_gate, w_up, b_up, w_down, b_down, norm_final_w)
```

```python
import functools

import jax
import jax.numpy as jnp
from jax import lax
from jax.experimental import pallas as pl
from jax.experimental.pallas import tpu as pltpu

D_MODEL = 1024
D_REC = 512
D_ATT = 512
N_HEADS = 8
D_HEAD = 64
N_EXPERTS = 32
TOP_K = 4
D_FF = 1024
PAGE_SIZE = 128
SWIGLU_LIMIT = 7.0
SWIGLU_ALPHA = 1.702
RMS_EPS = 1e-5

LANES = 128
SUBLANES = 8
NEG_BIG = -1e30
MIB = 1024 * 1024

F32 = jnp.float32
BF16 = jnp.bfloat16
HIGHEST = lax.Precision.HIGHEST

NT_DIMS = (((1,), (1,)), ((), ()))
TN_DIMS = (((0,), (0,)), ((), ()))


def _log_sigmoid(x):
    return jnp.minimum(x, 0.0) - jnp.log1p(jnp.exp(-jnp.abs(x)))


def _silu(x):
    return x * jax.nn.sigmoid(x)


def _params(sem, vmem_mib):
    return pltpu.CompilerParams(dimension_semantics=sem, vmem_limit_bytes=vmem_mib * MIB)


def _inproj_kernel(x_ref, nw_ref, wrec_ref, watt_ref, wfa_ref, wfat_ref, bfa_ref, bfat_ref,
                   zrec_ref, q_ref, k_ref, v_ref, kb_ref, vb_ref, lf_ref, lft_ref, *, kv_transposed):
    x = x_ref[...]
    r = lax.rsqrt(jnp.mean(x * x, axis=-1, keepdims=True) + RMS_EPS)
    xn = ((x * r) * nw_ref[...]).astype(BF16)
    zrec_ref[...] = jnp.dot(xn, wrec_ref[...], preferred_element_type=F32)
    za = jnp.dot(xn, watt_ref[...], preferred_element_type=F32)
    q_ref[...] = (za[:, :D_ATT] * (D_HEAD ** -0.5)).astype(BF16)
    k = za[:, D_ATT:2 * D_ATT]
    v = za[:, 2 * D_ATT:]
    if kv_transposed:
        k_ref[0] = k.T
        v_ref[0] = v.T
    else:
        k_ref[...] = k
        v_ref[...] = v
    kb_ref[...] = k.astype(BF16)
    vb_ref[...] = v.astype(BF16)
    fa = jnp.dot(xn, wfa_ref[...], preferred_element_type=F32)
    lf_ref[...] = _log_sigmoid(fa[:, :N_HEADS] + bfa_ref[...])
    fat = lax.dot_general(wfat_ref[...], xn, NT_DIMS, preferred_element_type=F32)
    lft_ref[...] = _log_sigmoid(fat[:N_HEADS, :] + bfat_ref[...])


def _inproj(x, nw, wrec, watt, wfa, wfat, bfa, bfat, tm, seq_len=None):
    n = x.shape[0]
    full = lambda a: pl.BlockSpec(a.shape, lambda i: (0,) * a.ndim)
    row = lambda w: pl.BlockSpec((tm, w), lambda i: (i, 0))
    if seq_len is None:
        kv_spec, kv_shape = row(D_ATT), jax.ShapeDtypeStruct((n, D_ATT), F32)
    else:
        nt = seq_len // tm
        kv_spec = pl.BlockSpec((1, D_ATT, tm), lambda i: (i // nt, 0, i % nt))
        kv_shape = jax.ShapeDtypeStruct((n // seq_len, D_ATT, seq_len), F32)
    return pl.pallas_call(
        functools.partial(_inproj_kernel, kv_transposed=seq_len is not None),
        grid=(n // tm,),
        in_specs=[row(D_MODEL), full(nw), full(wrec), full(watt), full(wfa), full(wfat), full(bfa), full(bfat)],
        out_specs=[row(4 * D_REC), row(D_ATT), kv_spec, kv_spec, row(D_ATT), row(D_ATT), row(N_HEADS),
                   pl.BlockSpec((N_HEADS, tm), lambda i: (0, i))],
        out_shape=[jax.ShapeDtypeStruct((n, 4 * D_REC), F32), jax.ShapeDtypeStruct((n, D_ATT), BF16),
                   kv_shape, kv_shape,
                   jax.ShapeDtypeStruct((n, D_ATT), BF16), jax.ShapeDtypeStruct((n, D_ATT), BF16),
                   jax.ShapeDtypeStruct((n, N_HEADS), F32), jax.ShapeDtypeStruct((N_HEADS, n), F32)],
        compiler_params=_params(("parallel",), 52),
        name="inproj",
    )(x, nw, wrec, watt, wfa, wfat, bfa, bfat)


def _cumsum_kernel(lf_ref, lft_ref, tri_ref, triu_ref, c_ref, ct_ref, *, tb):
    t = lf_ref.shape[1]
    tri = tri_ref[...]
    triu = triu_ref[...]
    carry = jnp.zeros((1, N_HEADS), F32)
    carry_t = jnp.zeros((N_HEADS, 1), F32)
    for j in range(t // tb):
        sl = slice(j * tb, (j + 1) * tb)
        cb = jnp.dot(tri, lf_ref[0, sl, :], precision=HIGHEST, preferred_element_type=F32) + carry
        c_ref[0, sl, :] = cb
        carry = cb[tb - 1:tb, :]
        cbt = jnp.dot(lft_ref[:, sl], triu, precision=HIGHEST, preferred_element_type=F32) + carry_t
        ct_ref[:, sl] = cbt
        carry_t = cbt[:, tb - 1:tb]


def _cumsum_logf(lf, lft, batch, t):
    tb = min(t, 512)
    idx = jnp.arange(tb)
    tri = (idx[:, None] >= idx[None, :]).astype(F32)
    return pl.pallas_call(
        functools.partial(_cumsum_kernel, tb=tb),
        grid=(batch,),
        in_specs=[pl.BlockSpec((1, t, N_HEADS), lambda b: (b, 0, 0)),
                  pl.BlockSpec((N_HEADS, t), lambda b: (0, b)),
                  pl.BlockSpec((tb, tb), lambda b: (0, 0)),
                  pl.BlockSpec((tb, tb), lambda b: (0, 0))],
        out_specs=[pl.BlockSpec((1, t, N_HEADS), lambda b: (b, 0, 0)),
                   pl.BlockSpec((N_HEADS, t), lambda b: (0, b))],
        out_shape=[jax.ShapeDtypeStruct((batch, t, N_HEADS), F32),
                   jax.ShapeDtypeStruct((N_HEADS, batch * t), F32)],
        compiler_params=_params(("parallel",), 32),
        name="cumsum_logf",
    )(lf.reshape(batch, t, N_HEADS), lft, tri, tri.T)


def _fox_prompt_kernel(q_ref, k_ref, v_ref, c_ref, ct_ref, o_ref, m_sc, acc_sc, *, tq, tk):
    p = pl.program_id(1)
    i = pl.program_id(2)
    ratio = tk // tq
    nrep = tk // LANES
    q = q_ref[...]
    lane = lax.broadcasted_iota(jnp.int32, (tq, LANES), 1)
    lane8 = lax.broadcasted_iota(jnp.int32, (tq, N_HEADS), 1)
    lane_k = lax.broadcasted_iota(jnp.int32, (tk, LANES), 1)
    cblk = c_ref[0]
    qs, cqs = [], []
    for hh in range(2):
        qs.append(jnp.where((lane // D_HEAD) == hh, q, jnp.zeros_like(q)))
        cq = jnp.sum(jnp.where(lane8 == 2 * p + hh, cblk, 0.0), axis=-1, keepdims=True)
        cqs.append(jnp.concatenate([jnp.broadcast_to(cq, (tq, LANES))] * nrep, axis=1))
    m_sc[...] = jnp.full(m_sc.shape, NEG_BIG, F32)
    acc_sc[...] = jnp.zeros(acc_sc.shape, F32)

    def step(j, masked):
        off = pl.multiple_of(j * tk, tk)
        kb = k_ref[pl.ds(off, tk), :]
        vb = v_ref[pl.ds(off, tk), :]
        for hh in range(2):
            va = jnp.where((lane_k // D_HEAD) == hh, vb, jnp.ones_like(vb))
            s = lax.dot_general(qs[hh], kb, NT_DIMS, preferred_element_type=F32)
            s = (s + cqs[hh]) - ct_ref[0, hh:hh + 1, pl.ds(off, tk)]
            if masked:
                row = lax.broadcasted_iota(jnp.int32, (tq, tk), 0) + (i % ratio) * tq
                col = lax.broadcasted_iota(jnp.int32, (tq, tk), 1)
                s = jnp.where(col <= row, s, NEG_BIG)
            m_prev = m_sc[hh]
            m_new = jnp.maximum(m_prev, jnp.max(s, axis=-1, keepdims=True))
            alpha = jnp.exp(m_prev - m_new)
            pr = jnp.exp(s - jnp.concatenate([m_new] * nrep, axis=1)).astype(BF16)
            acc_sc[hh] = alpha * acc_sc[hh] + jnp.dot(pr, va, preferred_element_type=F32)
            m_sc[hh] = m_new

    def body(j, carry):
        step(j, False)
        return carry

    lax.fori_loop(0, i // ratio, body, 0)
    step(i // ratio, True)
    outs = []
    for hh in range(2):
        a = acc_sc[hh]
        outs.append(a / pltpu.roll(a, D_HEAD, axis=1))
    o_ref[...] = jnp.where(lane < D_HEAD, outs[0], outs[1]).astype(o_ref.dtype)


def _fox_prompt(q, kb, vb, c, ct, batch, t, tq, tk):
    nq = t // tq
    ct4 = ct.reshape(N_HEADS // 2, 2, batch * t)
    return pl.pallas_call(
        functools.partial(_fox_prompt_kernel, tq=tq, tk=tk),
        grid=(batch, N_HEADS // 2, nq),
        in_specs=[pl.BlockSpec((tq, LANES), lambda b, p, i: (b * nq + i, p)),
                  pl.BlockSpec((t, LANES), lambda b, p, i: (b, p)),
                  pl.BlockSpec((t, LANES), lambda b, p, i: (b, p)),
                  pl.BlockSpec((1, tq, N_HEADS), lambda b, p, i: (b, i, 0)),
                  pl.BlockSpec((1, 2, t), lambda b, p, i: (p, 0, b))],
        out_specs=pl.BlockSpec((tq, LANES), lambda b, p, i: (b * nq + i, p)),
        out_shape=jax.ShapeDtypeStruct((batch * t, D_ATT), BF16),
        scratch_shapes=[pltpu.VMEM((2, tq, LANES), F32), pltpu.VMEM((2, tq, LANES), F32)],
        compiler_params=_params(("parallel", "parallel", "parallel"), 32),
        name="fox_prompt",
    )(q, kb, vb, c, ct4)


def _hgrn_chunk(zq, zf, zi, lb, gw, tri, s_sc, o_sc, *, chunk, sub, n_valid):
    fg = lb + (1.0 - lb) * jax.nn.sigmoid(zf)
    kk = 1.0 - fg
    g = jnp.log(fg)
    if n_valid < chunk:
        valid = lax.broadcasted_iota(jnp.int32, (chunk, D_REC), 0) < n_valid
        kk = jnp.where(valid, kk, 0.0)
        g = jnp.where(valid, g, 0.0)
    q = _silu(zq)
    b = jnp.dot(tri, g, precision=HIGHEST, preferred_element_type=F32)
    b_end = b[chunk - 1:chunk, :]
    qhat = (q * jnp.exp(b)).astype(BF16)
    khat = (kk * jnp.exp(b_end - b)).astype(BF16)
    vb = zi.astype(BF16)
    e_end = jnp.exp(b_end)
    s_old = [s_sc[h] for h in range(N_HEADS)]
    s_old_b = [s.astype(BF16) for s in s_old]
    for i in range(chunk // sub):
        r0, r1 = i * sub, (i + 1) * sub
        rho = b[r0 + sub // 2:r0 + sub // 2 + 1, :]
        qt = (q[r0:r1] * jnp.exp(b[r0:r1] - rho)).astype(BF16)
        kt = (kk[:r1] * jnp.exp(rho - b[:r1])).astype(BF16)
        row = lax.broadcasted_iota(jnp.int32, (sub, r1), 0) + r0
        col = lax.broadcasted_iota(jnp.int32, (sub, r1), 1)
        for h in range(N_HEADS):
            hs = slice(h * D_HEAD, (h + 1) * D_HEAD)
            a = lax.dot_general(qt[:, hs], kt[:, hs], NT_DIMS, preferred_element_type=F32)
            a = jnp.where(col <= row, a, 0.0).astype(BF16)
            o = (jnp.dot(a, vb[:r1, hs], preferred_element_type=F32)
                 + jnp.dot(qhat[r0:r1, hs], s_old_b[h], preferred_element_type=F32))
            ms = jnp.mean(o * o, axis=-1, keepdims=True)
            o_sc[r0:r1, hs] = (o * lax.rsqrt(ms + RMS_EPS)) * gw
    eye = (lax.broadcasted_iota(jnp.int32, (D_HEAD, D_HEAD), 0)
           == lax.broadcasted_iota(jnp.int32, (D_HEAD, D_HEAD), 1))
    for h in range(N_HEADS):
        hs = slice(h * D_HEAD, (h + 1) * D_HEAD)
        e_col = jnp.sum(jnp.where(eye, jnp.broadcast_to(e_end[:, hs], (D_HEAD, D_HEAD)), 0.0), axis=1, keepdims=True)
        s_sc[h] = e_col * s_old[h] + lax.dot_general(khat[:, hs], vb[:, hs], TN_DIMS, preferred_element_type=F32)


def _hgrn_pair(q, kk, b, zi, bd_mean, s_old, *, chunk, sub):
    b_end = b[chunk - 1:chunk, :]
    head0 = lax.broadcasted_iota(jnp.int32, (1, LANES), 1) < D_HEAD
    qhat = (q * jnp.exp(b)).astype(BF16)
    khat = (kk * jnp.exp(b_end - b)).astype(BF16)
    vb = zi.astype(BF16)
    zero_b = jnp.zeros((), BF16)
    v0 = jnp.where(head0, vb, zero_b)
    v1 = jnp.where(head0, zero_b, vb)
    s_old_b = s_old.astype(BF16)
    a0_rows, a1_rows = [], []
    for i in range(chunk // sub):
        r0, r1 = i * sub, (i + 1) * sub
        rho = b[r0 + sub // 2:r0 + sub // 2 + 1, :]
        qt = (q[r0:r1] * jnp.exp(b[r0:r1] - rho)).astype(BF16)
        kt = (kk * jnp.exp(rho - b)).astype(BF16)
        both = jnp.concatenate([jnp.where(head0, qt, zero_b), jnp.where(head0, zero_b, qt)], axis=0)
        a = lax.dot_general(both, kt, NT_DIMS, preferred_element_type=F32)
        a0_rows.append(a[:sub])
        a1_rows.append(a[sub:])
    causal = (lax.broadcasted_iota(jnp.int32, (chunk, chunk), 1)
              <= lax.broadcasted_iota(jnp.int32, (chunk, chunk), 0))
    a0 = jnp.where(causal, jnp.concatenate(a0_rows, axis=0), 0.0).astype(BF16)
    a1 = jnp.where(causal, jnp.concatenate(a1_rows, axis=0), 0.0).astype(BF16)
    o = jnp.dot(jnp.concatenate([a0, a1, qhat], axis=1), jnp.concatenate([v0, v1, s_old_b], axis=0),
                preferred_element_type=F32)
    o2 = o * o
    hi = o2.astype(BF16)
    lo = (o2 - hi.astype(F32)).astype(BF16)
    ms = jnp.dot(jnp.concatenate([hi, lo], axis=1), jnp.concatenate([bd_mean, bd_mean], axis=0),
                 preferred_element_type=F32)
    outs = [o * lax.rsqrt(ms + RMS_EPS)]
    ri = lax.broadcasted_iota(jnp.int32, (LANES, LANES), 0)
    ci = lax.broadcasted_iota(jnp.int32, (LANES, LANES), 1)
    e_col = jnp.sum(jnp.where(ri == ci, jnp.broadcast_to(jnp.exp(b_end), (LANES, LANES)), 0.0), axis=1, keepdims=True)
    upd = lax.dot_general(khat, vb, TN_DIMS, preferred_element_type=F32)
    s_new = e_col * s_old + jnp.where((ri // D_HEAD) == (ci // D_HEAD), upd, 0.0)
    return jnp.concatenate(outs, axis=0), s_new


def _hgrn_prompt_kernel(z_ref, lb_ref, gw_ref, tri_ref, bd_ref, o_ref, s_ref, s_sc, *, chunk, sub):
    j = pl.program_id(1)

    @pl.when(j == 0)
    def _():
        s_sc[...] = jnp.zeros(s_sc.shape, F32)

    s_olds = [s_sc[p] for p in range(N_HEADS // 2)]
    z = z_ref[...]
    lb = lb_ref[...]
    fg = lb + (1.0 - lb) * jax.nn.sigmoid(z[:, D_REC:2 * D_REC])
    kk = 1.0 - fg
    q = _silu(z[:, :D_REC])
    b = jnp.dot(tri_ref[...], jnp.log(fg), precision=HIGHEST, preferred_element_type=F32)
    outs = []
    for p in range(N_HEADS // 2):
        ps = slice(p * LANES, (p + 1) * LANES)
        o, s_new = _hgrn_pair(q[:, ps], kk[:, ps], b[:, ps], z[:, 2 * D_REC + p * LANES:2 * D_REC + (p + 1) * LANES],
                              bd_ref[...], s_olds[p], chunk=chunk, sub=sub)
        gate = _silu(z[:, 3 * D_REC + p * LANES:3 * D_REC + (p + 1) * LANES])
        outs.append(((o * gw_ref[...]) * gate).astype(o_ref.dtype))
        s_olds[p] = s_new
    for p in range(N_HEADS // 2):
        s_sc[p] = s_olds[p]
    o_ref[...] = jnp.concatenate(outs, axis=1)

    @pl.when(j == pl.num_programs(1) - 1)
    def _():
        for h in range(N_HEADS):
            d0 = (h % 2) * D_HEAD
            s_ref[0, h] = s_sc[h // 2][d0:d0 + D_HEAD, d0:d0 + D_HEAD]


def _hgrn_prompt(zrec, lb, gw, batch, t, chunk, sub):
    nc = t // chunk
    idx = jnp.arange(chunk)
    tri = (idx[:, None] >= idx[None, :]).astype(F32)
    li = jnp.arange(LANES) // D_HEAD
    bd_mean = ((li[:, None] == li[None, :]).astype(F32) / D_HEAD).astype(BF16)
    gw2 = jnp.concatenate([gw, gw], axis=1)
    return pl.pallas_call(
        functools.partial(_hgrn_prompt_kernel, chunk=chunk, sub=sub),
        grid=(batch, nc),
        in_specs=[pl.BlockSpec((chunk, 4 * D_REC), lambda b, j: (b * nc + j, 0)),
                  pl.BlockSpec((1, D_REC), lambda b, j: (0, 0)),
                  pl.BlockSpec((1, LANES), lambda b, j: (0, 0)),
                  pl.BlockSpec((chunk, chunk), lambda b, j: (0, 0)),
                  pl.BlockSpec((LANES, LANES), lambda b, j: (0, 0))],
        out_specs=[pl.BlockSpec((chunk, D_REC), lambda b, j: (b * nc + j, 0)),
                   pl.BlockSpec((1, N_HEADS, D_HEAD, D_HEAD), lambda b, j: (b, 0, 0, 0))],
        out_shape=[jax.ShapeDtypeStruct((batch * t, D_REC), BF16),
                   jax.ShapeDtypeStruct((batch, N_HEADS, D_HEAD, D_HEAD), F32)],
        scratch_shapes=[pltpu.VMEM((N_HEADS // 2, LANES, LANES), F32)],
        compiler_params=_params(("parallel", "arbitrary"), 32),
        name="hgrn_prompt",
    )(zrec, lb, gw2, tri, bd_mean)


def _hgrn_sample_kernel(z_ref, s0_ref, lb_ref, gw_ref, tri_ref, o_ref, s_ref, s_sc, o_sc, *, seqs, t, chunk):
    lb = lb_ref[...]
    gw = gw_ref[...]
    tri = tri_ref[...]
    pad = jnp.zeros((chunk - t, D_REC), F32)
    for s in range(seqs):
        z = z_ref[s]
        s_sc[...] = s0_ref[s]
        zq, zf, zi = (jnp.concatenate([z[:, u * D_REC:(u + 1) * D_REC], pad], axis=0) for u in range(3))
        _hgrn_chunk(zq, zf, zi, lb, gw, tri, s_sc, o_sc, chunk=chunk, sub=chunk, n_valid=t)
        o_ref[s] = (o_sc[:t, :] * _silu(z[:, 3 * D_REC:])).astype(o_ref.dtype)
        s_ref[s] = s_sc[...]


def _hgrn_sample(zrec, s0, lb, gw, batch, t, seqs):
    chunk = 8
    idx = jnp.arange(chunk)
    tri = (idx[:, None] >= idx[None, :]).astype(F32)
    return pl.pallas_call(
        functools.partial(_hgrn_sample_kernel, seqs=seqs, t=t, chunk=chunk),
        grid=(batch // seqs,),
        in_specs=[pl.BlockSpec((seqs, t, 4 * D_REC), lambda b: (b, 0, 0)),
                  pl.BlockSpec((seqs, N_HEADS, D_HEAD, D_HEAD), lambda b: (b, 0, 0, 0)),
                  pl.BlockSpec((1, D_REC), lambda b: (0, 0)),
                  pl.BlockSpec((1, D_HEAD), lambda b: (0, 0)),
                  pl.BlockSpec((chunk, chunk), lambda b: (0, 0))],
        out_specs=[pl.BlockSpec((seqs, t, D_REC), lambda b: (b, 0, 0)),
                   pl.BlockSpec((seqs, N_HEADS, D_HEAD, D_HEAD), lambda b: (b, 0, 0, 0))],
        out_shape=[jax.ShapeDtypeStruct((batch, t, D_REC), BF16),
                   jax.ShapeDtypeStruct((batch, N_HEADS, D_HEAD, D_HEAD), F32)],
        scratch_shapes=[pltpu.VMEM((N_HEADS, D_HEAD, D_HEAD), F32), pltpu.VMEM((chunk, D_REC), F32)],
        compiler_params=_params(("parallel",), 32),
        name="hgrn_sample",
    )(zrec.reshape(batch, t, 4 * D_REC), s0, lb, gw, tri)


def _fox_sample_kernel(pt_ref, q_ref, kn_ref, vn_ref, lft_ref, gs_ref, *rest, t, ppb):
    k_refs = rest[:ppb]
    v_refs = rest[ppb:2 * ppb]
    lf_refs = rest[2 * ppb:3 * ppb]
    o_ref = rest[3 * ppb]
    m_sc, l_sc, acc_sc, carry_sc, cq_sc = rest[3 * ppb + 1:]
    del pt_ref
    j = pl.program_id(1)
    nrow = t * N_HEADS
    lane = lax.broadcasted_iota(jnp.int32, (nrow, D_ATT), 1)
    rowi = lax.broadcasted_iota(jnp.int32, (nrow, D_ATT), 0)
    head_lanes = (lane // D_HEAD) == (rowi % N_HEADS)
    q = q_ref[0]
    qbd = jnp.concatenate([jnp.broadcast_to(q[u:u + 1, :], (N_HEADS, D_ATT)) for u in range(t)], axis=0)
    qbd = jnp.where(head_lanes, qbd, jnp.zeros_like(qbd))

    @pl.when(j == 0)
    def _():
        lft = lft_ref[0]
        cols = [lft[:, 0:1]]
        for u in range(1, t):
            cols.append(cols[-1] + lft[:, u:u + 1])
        cq_t = jnp.concatenate(cols, axis=1)
        cq_col = jnp.concatenate(cols, axis=0)
        cq_sc[...] = cq_col
        zpad = jnp.zeros((N_HEADS - t, D_ATT), F32)
        kn = jnp.concatenate([kn_ref[0], zpad], axis=0).astype(BF16)
        vn = jnp.concatenate([vn_ref[0], zpad], axis=0).astype(BF16)
        cq_t8 = jnp.concatenate([cq_t, jnp.zeros((N_HEADS, N_HEADS - t), F32)], axis=1)
        s = lax.dot_general(qbd, kn, NT_DIMS, preferred_element_type=F32)
        s = s + cq_col - jnp.concatenate([cq_t8] * t, axis=0)
        r2 = lax.broadcasted_iota(jnp.int32, (nrow, N_HEADS), 0) // N_HEADS
        c2 = lax.broadcasted_iota(jnp.int32, (nrow, N_HEADS), 1)
        s = jnp.where(c2 <= r2, s, NEG_BIG)
        m0 = jnp.max(s, axis=-1, keepdims=True)
        p0 = jnp.exp(s - m0)
        m_sc[...] = m0
        l_sc[...] = jnp.sum(p0, axis=-1, keepdims=True)
        acc_sc[...] = jnp.dot(p0.astype(BF16), vn, preferred_element_type=F32)
        carry_sc[...] = jnp.zeros(carry_sc.shape, F32)

    lf_all = jnp.concatenate([r[0] for r in lf_refs], axis=0)
    insuf = jnp.dot(lf_all, gs_ref[...], precision=HIGHEST, preferred_element_type=F32)
    carry = carry_sc[...]
    biases = []
    for u in range(ppb):
        biases.append(insuf[u * N_HEADS:(u + 1) * N_HEADS, :] + carry)
        carry = carry + jnp.sum(lf_refs[u][0], axis=-1, keepdims=True)
    carry_sc[...] = carry
    bias = jnp.concatenate(biases, axis=1)
    bias = jnp.concatenate([bias] * t, axis=0) + cq_sc[...]
    kcat = jnp.concatenate([r[0].astype(BF16) for r in k_refs], axis=1)
    vcat = jnp.concatenate([r[0].astype(BF16) for r in v_refs], axis=1)
    s = jnp.dot(qbd, kcat, preferred_element_type=F32) + bias
    m_prev = m_sc[...]
    m_new = jnp.maximum(m_prev, jnp.max(s, axis=-1, keepdims=True))
    alpha = jnp.exp(m_prev - m_new)
    pr = jnp.exp(s - m_new)
    l_sc[...] = alpha * l_sc[...] + jnp.sum(pr, axis=-1, keepdims=True)
    acc_sc[...] = alpha * acc_sc[...] + lax.dot_general(pr.astype(BF16), vcat, NT_DIMS, preferred_element_type=F32)
    m_sc[...] = m_new

    @pl.when(j == pl.num_programs(1) - 1)
    def _():
        o = jnp.where(head_lanes, acc_sc[...] / l_sc[...], 0.0)
        o_ref[0] = jnp.sum(o.reshape(t, N_HEADS, D_ATT), axis=1).astype(o_ref.dtype)


def _fox_sample(page_table, q, kn, vn, lft, cache_k, cache_v, cache_lft, batch, t, ppb):
    n_pages = page_table.shape[1]
    nsteps = n_pages // ppb
    idx = jnp.arange(PAGE_SIZE)
    gs = (idx[:, None] > idx[None, :]).astype(F32)
    seq = lambda w: pl.BlockSpec((1, t, w), lambda b, j, pt: (b, 0, 0))

    def page(u, shape):
        return pl.BlockSpec((1,) + shape, lambda b, j, pt: (pt[b, n_pages - 1 - (j * ppb + u)], 0, 0))

    in_specs = ([seq(D_ATT), seq(D_ATT), seq(D_ATT),
                 pl.BlockSpec((1, N_HEADS, t), lambda b, j, pt: (b, 0, 0)),
                 pl.BlockSpec((PAGE_SIZE, PAGE_SIZE), lambda b, j, pt: (0, 0))]
                + [page(u, (D_ATT, PAGE_SIZE)) for u in range(ppb)]
                + [page(u, (D_ATT, PAGE_SIZE)) for u in range(ppb)]
                + [page(u, (N_HEADS, PAGE_SIZE)) for u in range(ppb)])
    nrow = t * N_HEADS
    grid_spec = pltpu.PrefetchScalarGridSpec(
        num_scalar_prefetch=1,
        grid=(batch, nsteps),
        in_specs=in_specs,
        out_specs=pl.BlockSpec((1, t, D_ATT), lambda b, j, pt: (b, 0, 0)),
        scratch_shapes=[pltpu.VMEM((nrow, 1), F32), pltpu.VMEM((nrow, 1), F32), pltpu.VMEM((nrow, D_ATT), F32),
                        pltpu.VMEM((N_HEADS, 1), F32), pltpu.VMEM((nrow, 1), F32)],
    )
    return pl.pallas_call(
        functools.partial(_fox_sample_kernel, t=t, ppb=ppb),
        grid_spec=grid_spec,
        out_shape=jax.ShapeDtypeStruct((batch, t, D_ATT), BF16),
        compiler_params=_params(("parallel", "arbitrary"), 40),
        name="fox_sample",
    )(page_table, q.reshape(batch, t, D_ATT), kn.reshape(batch, t, D_ATT), vn.reshape(batch, t, D_ATT), lft, gs,
      *([cache_k] * ppb), *([cache_v] * ppb), *([cache_lft] * ppb))


def _outproj_router_kernel(x_ref, orec_ref, oatt_ref, wo_ref, nw_ref, wr_ref, br_ref, cnt0_ref, tril_ref,
                           h_ref, xn_ref, idx_ref, gate_ref, rank_ref, cnt_ref, base_sc):
    i = pl.program_id(0)

    @pl.when(i == 0)
    def _():
        base_sc[...] = cnt0_ref[...]

    tm = x_ref.shape[0]
    h = (x_ref[...]
         + jnp.dot(orec_ref[...], wo_ref[:D_REC, :], preferred_element_type=F32)
         + jnp.dot(oatt_ref[...], wo_ref[D_REC:, :], preferred_element_type=F32))
    h_ref[...] = h
    r = lax.rsqrt(jnp.mean(h * h, axis=-1, keepdims=True) + RMS_EPS)
    xn = (h * r) * nw_ref[...]
    xn_ref[...] = xn
    x_hi = xn.astype(BF16)
    x_lo = (xn - x_hi.astype(F32)).astype(BF16)
    logits = (jnp.dot(x_hi, wr_ref[0], preferred_element_type=F32)
              + (jnp.dot(x_lo, wr_ref[0], preferred_element_type=F32)
                 + jnp.dot(x_hi, wr_ref[1], preferred_element_type=F32))) + br_ref[...]
    lane = lax.broadcasted_iota(jnp.int32, (tm, LANES), 1)
    vals, hots = [], []
    idx_out = jnp.zeros((tm, LANES), jnp.int32)
    for k in range(TOP_K):
        m = jnp.max(logits, axis=-1, keepdims=True)
        sel = jnp.min(jnp.where(logits == m, lane, LANES), axis=-1, keepdims=True)
        hot = lane == sel
        vals.append(m)
        hots.append(hot)
        idx_out = jnp.where(lane == k, sel, idx_out)
        logits = jnp.where(hot, -jnp.inf, logits)
    idx_ref[...] = idx_out
    es = [jnp.exp(v - vals[0]) for v in vals]
    denom = es[0] + es[1] + es[2] + es[3]
    gate_out = jnp.zeros((tm, LANES), F32)
    for k in range(TOP_K):
        gate_out = jnp.where(lane == k, es[k] / denom, gate_out)
    gate_ref[...] = gate_out
    base = base_sc[...]
    rank_out = jnp.zeros((tm, LANES), jnp.int32)
    for k in range(TOP_K):
        hot_f = hots[k].astype(F32)
        before = jnp.dot(tril_ref[...], hot_f.astype(BF16), preferred_element_type=F32) + base
        rank = jnp.sum(jnp.where(hots[k], before, 0.0), axis=-1, keepdims=True)
        rank_out = jnp.where(lane == k, rank.astype(jnp.int32), rank_out)
        base = base + jnp.sum(hot_f, axis=0, keepdims=True)
    rank_ref[...] = rank_out
    base_sc[...] = base
    cnt_ref[...] = base


def _outproj_router(x, orec, oatt, wo, nw, wr, br, cnt0, tm):
    n = x.shape[0]
    idx = jnp.arange(tm)
    tril = (idx[:, None] > idx[None, :]).astype(BF16)
    full = lambda a: pl.BlockSpec(a.shape, lambda i: (0,) * a.ndim)
    row = lambda w: pl.BlockSpec((tm, w), lambda i: (i, 0))
    return pl.pallas_call(
        _outproj_router_kernel,
        grid=(n // tm,),
        in_specs=[row(D_MODEL), row(D_REC), row(D_ATT), full(wo), full(nw), full(wr), full(br), full(cnt0), full(tril)],
        out_specs=[row(D_MODEL), row(D_MODEL), row(LANES), row(LANES), row(LANES),
                   pl.BlockSpec((1, LANES), lambda i: (0, 0))],
        out_shape=[jax.ShapeDtypeStruct((n, D_MODEL), F32), jax.ShapeDtypeStruct((n, D_MODEL), F32),
                   jax.ShapeDtypeStruct((n, LANES), jnp.int32), jax.ShapeDtypeStruct((n, LANES), F32),
                   jax.ShapeDtypeStruct((n, LANES), jnp.int32), jax.ShapeDtypeStruct((1, LANES), F32)],
        scratch_shapes=[pltpu.VMEM((1, LANES), F32)],
        compiler_params=_params(("arbitrary",), 40),
        name="outproj_router",
    )(x, orec, oatt, wo, nw, wr, br, cnt0, tril)


def _dispatch_kernel(gend_ref, gsize_ref, total_ref, dest_ref, xp_ref, xs_ref, out_ref, zbuf, sem, zsem,
                     *, n_prompt_tiles, tile_rows):
    i = pl.program_id(0)
    n_tiles = out_ref.shape[0] // tile_rows

    @pl.when(i == 0)
    def _():
        zbuf[...] = jnp.zeros(zbuf.shape, zbuf.dtype)

        def zero_tile(start):
            cp = pltpu.make_async_copy(zbuf, out_ref.at[pl.ds(pl.multiple_of(start, tile_rows), tile_rows), :], zsem)
            cp.start()
            cp.wait()

        for e in range(N_EXPERTS):
            @pl.when(gsize_ref[e] > 0)
            def _():
                zero_tile(gend_ref[e] - tile_rows)

        def tail(tt, carry):
            zero_tile(tt * tile_rows)
            return carry

        lax.fori_loop(total_ref[0] // tile_rows, n_tiles, tail, 0)

    def scatter(src_ref):
        n = src_ref.shape[0]

        def issue(g, carry):
            r0 = pl.multiple_of(g * SUBLANES, SUBLANES)
            for u in range(SUBLANES):
                for k in range(TOP_K):
                    pltpu.make_async_copy(src_ref.at[pl.ds(r0 + u, 1), :],
                                          out_ref.at[pl.ds(dest_ref[0, 0, (r0 + u) * TOP_K + k], 1), :],
                                          sem).start(priority=k % 2)
            return carry

        lax.fori_loop(0, n // SUBLANES, issue, 0)
        for k in range(TOP_K):
            pltpu.make_async_copy(src_ref, out_ref.at[pl.ds(0, n), :], sem).wait()

    @pl.when(i < n_prompt_tiles)
    def _():
        scatter(xp_ref)

    @pl.when(i == n_prompt_tiles)
    def _():
        scatter(xs_ref)


def _dispatch(group_end, group_size, dest_p, dest_s, xn_p, xn_s, n_rows, tm, tile_rows):
    n_p, n_s = xn_p.shape[0], xn_s.shape[0]
    npt = n_p // tm
    dest = jnp.concatenate([dest_p.reshape(npt, tm * TOP_K),
                            jnp.pad(dest_s, (0, (tm - n_s) * TOP_K)).reshape(1, tm * TOP_K)], axis=0)
    grid_spec = pltpu.PrefetchScalarGridSpec(
        num_scalar_prefetch=3,
        grid=(npt + 1,),
        in_specs=[pl.BlockSpec((1, 1, tm * TOP_K), lambda i, *_: (i, 0, 0), memory_space=pltpu.SMEM),
                  pl.BlockSpec((tm, D_MODEL), lambda i, *_: (jnp.minimum(i, npt - 1), 0)),
                  pl.BlockSpec((n_s, D_MODEL), lambda i, *_: (0, 0))],
        out_specs=pl.BlockSpec(memory_space=pl.ANY),
        scratch_shapes=[pltpu.VMEM((tile_rows, D_MODEL), F32), pltpu.SemaphoreType.DMA(()),
                        pltpu.SemaphoreType.DMA(())],
    )
    return pl.pallas_call(
        functools.partial(_dispatch_kernel, n_prompt_tiles=npt, tile_rows=tile_rows),
        grid_spec=grid_spec,
        out_shape=jax.ShapeDtypeStruct((n_rows, D_MODEL), F32),
        compiler_params=_params(("arbitrary",), 32),
        name="moe_dispatch",
    )(group_end, group_size, group_end[-1:], dest.reshape(npt + 1, 1, tm * TOP_K), xn_p, xn_s)


def _expert_mlp_kernel(te_ref, tv_ref, tf_ref, x_ref, wg_ref, bg_ref, wu_ref, bu_ref, wd_ref, bd_ref, y_ref,
                       wg_sc, wu_sc, wd_sc):
    i = pl.program_id(0)
    del te_ref

    @pl.when(tf_ref[i] != 0)
    def _():
        wg_sc[...] = wg_ref[0].astype(BF16)
        wu_sc[...] = wu_ref[0].astype(BF16)
        wd_sc[...] = wd_ref[0].astype(BF16)

    @pl.when(tv_ref[i] != 0)
    def _():
        x = x_ref[...].astype(BF16)
        gate = jnp.minimum(jnp.dot(x, wg_sc[...], preferred_element_type=F32) + bg_ref[0], SWIGLU_LIMIT)
        up = jnp.clip(jnp.dot(x, wu_sc[...], preferred_element_type=F32) + bu_ref[0], -SWIGLU_LIMIT, SWIGLU_LIMIT)
        glu = gate * jax.nn.sigmoid(gate * SWIGLU_ALPHA)
        mid = ((up + 1.0) * glu).astype(BF16)
        y_ref[...] = jnp.dot(mid, wd_sc[...], preferred_element_type=F32) + bd_ref[0]

    @pl.when(tv_ref[i] == 0)
    def _():
        y_ref[...] = jnp.zeros(y_ref.shape, y_ref.dtype)


def _expert_mlp(tile_expert, tile_valid, tile_first, xs, wg, bg, wu, bu, wd, bd, tm):
    p = xs.shape[0]
    wspec = lambda: pl.BlockSpec((1, D_MODEL, D_FF), lambda i, te, tv, tf: (te[i], 0, 0))
    bspec = lambda: pl.BlockSpec((1, 1, D_FF), lambda i, te, tv, tf: (te[i], 0, 0))
    grid_spec = pltpu.PrefetchScalarGridSpec(
        num_scalar_prefetch=3,
        grid=(p // tm,),
        in_specs=[pl.BlockSpec((tm, D_MODEL), lambda i, te, tv, tf: (i * tv[i], 0)),
                  wspec(), bspec(), wspec(), bspec(), wspec(), bspec()],
        out_specs=pl.BlockSpec((tm, D_MODEL), lambda i, te, tv, tf: (i, 0)),
        scratch_shapes=[pltpu.VMEM((D_MODEL, D_FF), BF16), pltpu.VMEM((D_MODEL, D_FF), BF16),
                        pltpu.VMEM((D_FF, D_MODEL), BF16)],
    )
    return pl.pallas_call(
        _expert_mlp_kernel,
        grid_spec=grid_spec,
        out_shape=jax.ShapeDtypeStruct((p, D_MODEL), F32),
        compiler_params=_params(("arbitrary",), 56),
        name="expert_mlp",
    )(tile_expert, tile_valid, tile_first, xs, wg, bg, wu, bu, wd, bd)


def _combine_kernel(dest_ref, gate_ref, h_ref, nw_ref, ys_ref, y_ref, buf, sem):
    tm = h_ref.shape[0]

    def issue(g, carry):
        r0 = pl.multiple_of(g * SUBLANES, SUBLANES)
        for u in range(SUBLANES):
            for k in range(TOP_K):
                pltpu.make_async_copy(ys_ref.at[pl.ds(dest_ref[0, 0, (r0 + u) * TOP_K + k], 1), :],
                                      buf.at[k, pl.ds(r0 + u, 1), :], sem).start(priority=k % 2)
        return carry

    lax.fori_loop(0, tm // SUBLANES, issue, 0)
    for k in range(TOP_K):
        pltpu.make_async_copy(ys_ref.at[pl.ds(0, tm), :], buf.at[k], sem).wait()
    gates = gate_ref[...]
    moe = gates[:, 0:1] * buf[0]
    for k in range(1, TOP_K):
        moe = moe + gates[:, k:k + 1] * buf[k]
    h = h_ref[...] + moe
    r = lax.rsqrt(jnp.mean(h * h, axis=-1, keepdims=True) + RMS_EPS)
    y_ref[...] = (h * r) * nw_ref[...]


def _combine(dest, gates, h, nw, ys, tm):
    n = h.shape[0]
    return pl.pallas_call(
        _combine_kernel,
        grid=(n // tm,),
        in_specs=[pl.BlockSpec((1, 1, tm * TOP_K), lambda i: (i, 0, 0), memory_space=pltpu.SMEM),
                  pl.BlockSpec((tm, LANES), lambda i: (i, 0)),
                  pl.BlockSpec((tm, D_MODEL), lambda i: (i, 0)),
                  pl.BlockSpec((1, D_MODEL), lambda i: (0, 0)),
                  pl.BlockSpec(memory_space=pl.ANY)],
        out_specs=pl.BlockSpec((tm, D_MODEL), lambda i: (i, 0)),
        out_shape=jax.ShapeDtypeStruct((n, D_MODEL), F32),
        scratch_shapes=[pltpu.VMEM((TOP_K, tm, D_MODEL), F32), pltpu.SemaphoreType.DMA(())],
        compiler_params=_params(("arbitrary",), 32),
        name="moe_combine",
    )(dest.reshape(n // tm, 1, tm * TOP_K), gates, h, nw, ys)


TM_TOKENS = 512
TM_EXPERT = 512
TM_COMBINE = 256
TQ_PROMPT = 512
TK_PROMPT = 512
HGRN_CHUNK = 128
HGRN_SUB = 32
PAGES_PER_STEP = 8
HGRN_SAMPLE_SEQS = 8


def _step(x_prompt, x_sample, cache_k, cache_v, cache_logf, state_hgrn, page_table,
          norm_mix_w, w_in, b_fa, lb_param, gnorm_w, w_out, norm_ffn_w,
          w_router, b_router, w_gate, b_gate, w_up, b_up, w_down, b_down, norm_final_w):
    batch, t, _ = x_prompt.shape
    dec_batch, dec_t, _ = x_sample.shape
    n_p, n_s = batch * t, dec_batch * dec_t
    n_pool = cache_k.shape[1]

    w_in_b = w_in[0].astype(BF16)
    wrec = w_in_b[:, :4 * D_REC]
    watt = w_in_b[:, 4 * D_REC:4 * D_REC + 3 * D_ATT]
    wfa = jnp.pad(w_in_b[:, 4 * D_REC + 3 * D_ATT:], ((0, 0), (0, LANES - N_HEADS)))
    wfat = jnp.pad(w_in_b[:, 4 * D_REC + 3 * D_ATT:].T, ((0, 16 - N_HEADS), (0, 0)))
    bfa = b_fa[0].reshape(1, N_HEADS)
    bfat = b_fa[0].reshape(N_HEADS, 1)
    nw_mix = norm_mix_w[0].reshape(1, D_MODEL)
    nw_ffn = norm_ffn_w[0].reshape(1, D_MODEL)
    nw_fin = norm_final_w.reshape(1, D_MODEL)
    lb = jnp.cumsum(jax.nn.softmax(lb_param.astype(F32), axis=0), axis=0)[0].reshape(1, D_REC)
    gw = gnorm_w.reshape(1, D_HEAD)
    wo = w_out[0].astype(BF16)
    wr_f = jnp.pad(w_router[0], ((0, 0), (0, LANES - N_EXPERTS)))
    wr_hi = wr_f.astype(BF16)
    wr = jnp.stack([wr_hi, (wr_f - wr_hi.astype(F32)).astype(BF16)])
    br = jnp.pad(b_router[0], (0, LANES - N_EXPERTS), constant_values=NEG_BIG).reshape(1, LANES)
    wg, wu, wd = w_gate[0], w_up[0], w_down[0]
    bg, bu, bd = (b[0].reshape(N_EXPERTS, 1, -1) for b in (b_gate, b_up, b_down))

    xp = x_prompt.reshape(n_p, D_MODEL)
    xs = x_sample.reshape(n_s, D_MODEL)
    tm_p, tm_s = min(TM_TOKENS, n_p), min(TM_TOKENS, n_s)
    proj = functools.partial(_inproj, nw=nw_mix, wrec=wrec, watt=watt, wfa=wfa, wfat=wfat, bfa=bfa, bfat=bfat)
    zrec_p, q_p, kt_p, vt_p, kb_p, vb_p, lf_p, lft_p = proj(xp, tm=tm_p, seq_len=t)
    zrec_s, q_s, k_s, v_s, _, _, lf_s, lft_s = proj(xs, tm=tm_s)

    c_p, ct_p = _cumsum_logf(lf_p, lft_p, batch, t)
    oatt_p = _fox_prompt(q_p, kb_p, vb_p, c_p, ct_p, batch, t, TQ_PROMPT, TK_PROMPT)
    orec_p, state_p = _hgrn_prompt(zrec_p, lb, gw, batch, t, HGRN_CHUNK, HGRN_SUB)

    orec_s, state_s = _hgrn_sample(zrec_s, state_hgrn[0], lb, gw, dec_batch, dec_t, HGRN_SAMPLE_SEQS)
    lft_s3 = lft_s.reshape(N_HEADS, dec_batch, dec_t).transpose(1, 0, 2)
    cache_lft = cache_logf[0].transpose(0, 2, 1)
    oatt_s = _fox_sample(page_table, q_s, k_s, v_s, lft_s3,
                         cache_k[0].transpose(0, 2, 3, 1).reshape(n_pool, D_ATT, PAGE_SIZE),
                         cache_v[0].transpose(0, 2, 3, 1).reshape(n_pool, D_ATT, PAGE_SIZE),
                         cache_lft, dec_batch, dec_t, PAGES_PER_STEP)

    route = functools.partial(_outproj_router, wo=wo, nw=nw_ffn, wr=wr, br=br)
    h_p, xn_p, idx_p, gate_p, rank_p, cnt_p = route(xp, orec_p, oatt_p, cnt0=jnp.zeros((1, LANES), F32), tm=tm_p)
    h_s, xn_s, idx_s, gate_s, rank_s, cnt = route(xs, orec_s.reshape(n_s, D_REC), oatt_s.reshape(n_s, D_ATT),
                                                  cnt0=cnt_p, tm=tm_s)

    counts = cnt[0, :N_EXPERTS].astype(jnp.int32)
    padded = ((counts + TM_EXPERT - 1) // TM_EXPERT) * TM_EXPERT
    group_end = jnp.cumsum(padded)
    group_start = group_end - padded
    n_rows = (n_p + n_s) * TOP_K + N_EXPERTS * TM_EXPERT
    tile_start = jnp.arange(n_rows // TM_EXPERT, dtype=jnp.int32) * TM_EXPERT
    tile_expert = jnp.minimum(jnp.sum((tile_start[:, None] >= group_end[None, :]).astype(jnp.int32), axis=1),
                              N_EXPERTS - 1)
    tile_valid = (tile_start < group_end[-1]).astype(jnp.int32)
    tile_first = jnp.concatenate([jnp.ones((1,), jnp.int32),
                                  (tile_expert[1:] != tile_expert[:-1]).astype(jnp.int32)])
    dest_p = (group_start[idx_p[:, :TOP_K]] + rank_p[:, :TOP_K]).reshape(-1)
    dest_s = (group_start[idx_s[:, :TOP_K]] + rank_s[:, :TOP_K]).reshape(-1)

    x_sorted = _dispatch(group_end.astype(jnp.int32), padded, dest_p, dest_s, xn_p, xn_s, n_rows, tm_p, TM_EXPERT)
    y_sorted = _expert_mlp(tile_expert, tile_valid, tile_first, x_sorted, wg, bg, wu, bu, wd, bd, TM_EXPERT)
    y_p = _combine(dest_p, gate_p, h_p, nw_fin, y_sorted, min(TM_COMBINE, n_p))
    y_s = _combine(dest_s, gate_s, h_s, nw_fin, y_sorted, min(TM_COMBINE, n_s))

    heads = lambda a, b_, t_: a.reshape(1, b_, t_, N_HEADS, D_HEAD)
    heads_t = lambda a: a.reshape(1, batch, N_HEADS, D_HEAD, t).transpose(0, 1, 4, 2, 3)
    return (y_p.reshape(batch, t, D_MODEL), y_s.reshape(dec_batch, dec_t, D_MODEL),
            heads_t(kt_p), heads_t(vt_p), lf_p.reshape(1, batch, t, N_HEADS), state_p[None],
            heads(k_s, dec_batch, dec_t), heads(v_s, dec_batch, dec_t), lf_s.reshape(1, dec_batch, dec_t, N_HEADS),
            state_s[None])


def kernel(x_prompt, x_sample, cache_k, cache_v, cache_logf, state_hgrn, page_table, norm_mix_w, w_in, b_fa, lb_param, gnorm_w, w_out, norm_ffn_w, w_router, b_router, w_gate, b_gate, w_up, b_up, w_down, b_down, norm_final_w):
    return _step(x_prompt, x_sample, cache_k, cache_v, cache_logf, state_hgrn, page_table,
                 norm_mix_w, w_in, b_fa, lb_param, gnorm_w, w_out, norm_ffn_w,
                 w_router, b_router, w_gate, b_gate, w_up, b_up, w_down, b_down, norm_final_w)
```

```python
import functools

import jax
import jax.numpy as jnp
from jax import lax
from jax.experimental import pallas as pl
from jax.experimental.pallas import tpu as pltpu

D_MODEL = 1024
D_REC = 512
D_ATT = 512
N_HEADS = 8
D_HEAD = 64
N_EXPERTS = 32
TOP_K = 4
D_FF = 1024
PAGE_SIZE = 128
SWIGLU_LIMIT = 7.0
SWIGLU_ALPHA = 1.702
RMS_EPS = 1e-5

LANES = 128
SUBLANES = 8
NEG_BIG = -1e30
MIB = 1024 * 1024

F32 = jnp.float32
BF16 = jnp.bfloat16
HIGHEST = lax.Precision.HIGHEST

NT_DIMS = (((1,), (1,)), ((), ()))
TN_DIMS = (((0,), (0,)), ((), ()))


def _log_sigmoid(x):
    return jnp.minimum(x, 0.0) - jnp.log1p(jnp.exp(-jnp.abs(x)))


def _silu(x):
    return x * jax.nn.sigmoid(x)


def _params(sem, vmem_mib):
    return pltpu.CompilerParams(dimension_semantics=sem, vmem_limit_bytes=vmem_mib * MIB)


def _inproj_kernel(x_ref, nw_ref, wrec_ref, watt_ref, wfa_ref, wfat_ref, bfa_ref, bfat_ref,
                   zrec_ref, q_ref, k_ref, v_ref, kb_ref, vb_ref, lf_ref, lft_ref, *, kv_transposed):
    x = x_ref[...]
    r = lax.rsqrt(jnp.mean(x * x, axis=-1, keepdims=True) + RMS_EPS)
    xn = ((x * r) * nw_ref[...]).astype(BF16)
    zrec_ref[...] = jnp.dot(xn, wrec_ref[...], preferred_element_type=F32)
    za = jnp.dot(xn, watt_ref[...], preferred_element_type=F32)
    q_ref[...] = (za[:, :D_ATT] * (D_HEAD ** -0.5)).astype(BF16)
    k = za[:, D_ATT:2 * D_ATT]
    v = za[:, 2 * D_ATT:]
    if kv_transposed:
        k_ref[0] = k.T
        v_ref[0] = v.T
    else:
        k_ref[...] = k
        v_ref[...] = v
    kb_ref[...] = k.astype(BF16)
    vb_ref[...] = v.astype(BF16)
    fa = jnp.dot(xn, wfa_ref[...], preferred_element_type=F32)
    lf_ref[...] = _log_sigmoid(fa[:, :N_HEADS] + bfa_ref[...])
    fat = lax.dot_general(wfat_ref[...], xn, NT_DIMS, preferred_element_type=F32)
    lft_ref[...] = _log_sigmoid(fat[:N_HEADS, :] + bfat_ref[...])


def _inproj(x, nw, wrec, watt, wfa, wfat, bfa, bfat, tm, seq_len=None):
    n = x.shape[0]
    full = lambda a: pl.BlockSpec(a.shape, lambda i: (0,) * a.ndim)
    row = lambda w: pl.BlockSpec((tm, w), lambda i: (i, 0))
    if seq_len is None:
        kv_spec, kv_shape = row(D_ATT), jax.ShapeDtypeStruct((n, D_ATT), F32)
    else:
        nt = seq_len // tm
        kv_spec = pl.BlockSpec((1, D_ATT, tm), lambda i: (i // nt, 0, i % nt))
        kv_shape = jax.ShapeDtypeStruct((n // seq_len, D_ATT, seq_len), F32)
    return pl.pallas_call(
        functools.partial(_inproj_kernel, kv_transposed=seq_len is not None),
        grid=(n // tm,),
        in_specs=[row(D_MODEL), full(nw), full(wrec), full(watt), full(wfa), full(wfat), full(bfa), full(bfat)],
        out_specs=[row(4 * D_REC), row(D_ATT), kv_spec, kv_spec, row(D_ATT), row(D_ATT), row(N_HEADS),
                   pl.BlockSpec((N_HEADS, tm), lambda i: (0, i))],
        out_shape=[jax.ShapeDtypeStruct((n, 4 * D_REC), F32), jax.ShapeDtypeStruct((n, D_ATT), BF16),
                   kv_shape, kv_shape,
                   jax.ShapeDtypeStruct((n, D_ATT), BF16), jax.ShapeDtypeStruct((n, D_ATT), BF16),
                   jax.ShapeDtypeStruct((n, N_HEADS), F32), jax.ShapeDtypeStruct((N_HEADS, n), F32)],
        compiler_params=_params(("parallel",), 52),
        name="inproj",
    )(x, nw, wrec, watt, wfa, wfat, bfa, bfat)


def _cumsum_kernel(lf_ref, lft_ref, tri_ref, triu_ref, c_ref, ct_ref, *, tb):
    t = lf_ref.shape[1]
    tri = tri_ref[...]
    triu = triu_ref[...]
    carry = jnp.zeros((1, N_HEADS), F32)
    carry_t = jnp.zeros((N_HEADS, 1), F32)
    for j in range(t // tb):
        sl = slice(j * tb, (j + 1) * tb)
        cb = jnp.dot(tri, lf_ref[0, sl, :], precision=HIGHEST, preferred_element_type=F32) + carry
        c_ref[0, sl, :] = cb
        carry = cb[tb - 1:tb, :]
        cbt = jnp.dot(lft_ref[:, sl], triu, precision=HIGHEST, preferred_element_type=F32) + carry_t
        ct_ref[:, sl] = cbt
        carry_t = cbt[:, tb - 1:tb]


def _cumsum_logf(lf, lft, batch, t):
    tb = min(t, 512)
    idx = jnp.arange(tb)
    tri = (idx[:, None] >= idx[None, :]).astype(F32)
    return pl.pallas_call(
        functools.partial(_cumsum_kernel, tb=tb),
        grid=(batch,),
        in_specs=[pl.BlockSpec((1, t, N_HEADS), lambda b: (b, 0, 0)),
                  pl.BlockSpec((N_HEADS, t), lambda b: (0, b)),
                  pl.BlockSpec((tb, tb), lambda b: (0, 0)),
                  pl.BlockSpec((tb, tb), lambda b: (0, 0))],
        out_specs=[pl.BlockSpec((1, t, N_HEADS), lambda b: (b, 0, 0)),
                   pl.BlockSpec((N_HEADS, t), lambda b: (0, b))],
        out_shape=[jax.ShapeDtypeStruct((batch, t, N_HEADS), F32),
                   jax.ShapeDtypeStruct((N_HEADS, batch * t), F32)],
        compiler_params=_params(("parallel",), 32),
        name="cumsum_logf",
    )(lf.reshape(batch, t, N_HEADS), lft, tri, tri.T)


def _fox_prompt_kernel(q_ref, k_ref, v_ref, c_ref, ct_ref, o_ref, m_sc, acc_sc, *, tq, tk):
    p = pl.program_id(1)
    i = pl.program_id(2)
    ratio = tk // tq
    nrep = tk // LANES
    q = q_ref[...]
    lane = lax.broadcasted_iota(jnp.int32, (tq, LANES), 1)
    lane8 = lax.broadcasted_iota(jnp.int32, (tq, N_HEADS), 1)
    lane_k = lax.broadcasted_iota(jnp.int32, (tk, LANES), 1)
    cblk = c_ref[0]
    qs, cqs = [], []
    for hh in range(2):
        qs.append(jnp.where((lane // D_HEAD) == hh, q, jnp.zeros_like(q)))
        cq = jnp.sum(jnp.where(lane8 == 2 * p + hh, cblk, 0.0), axis=-1, keepdims=True)
        cqs.append(jnp.concatenate([jnp.broadcast_to(cq, (tq, LANES))] * nrep, axis=1))
    m_sc[...] = jnp.full(m_sc.shape, NEG_BIG, F32)
    acc_sc[...] = jnp.zeros(acc_sc.shape, F32)

    def step(j, masked):
        off = pl.multiple_of(j * tk, tk)
        kb = k_ref[pl.ds(off, tk), :]
        vb = v_ref[pl.ds(off, tk), :]
        for hh in range(2):
            va = jnp.where((lane_k // D_HEAD) == hh, vb, jnp.ones_like(vb))
            s = lax.dot_general(qs[hh], kb, NT_DIMS, preferred_element_type=F32)
            s = (s + cqs[hh]) - ct_ref[0, hh:hh + 1, pl.ds(off, tk)]
            if masked:
                row = lax.broadcasted_iota(jnp.int32, (tq, tk), 0) + (i % ratio) * tq
                col = lax.broadcasted_iota(jnp.int32, (tq, tk), 1)
                s = jnp.where(col <= row, s, NEG_BIG)
            m_prev = m_sc[hh]
            m_new = jnp.maximum(m_prev, jnp.max(s, axis=-1, keepdims=True))
            alpha = jnp.exp(m_prev - m_new)
            pr = jnp.exp(s - jnp.concatenate([m_new] * nrep, axis=1)).astype(BF16)
            acc_sc[hh] = alpha * acc_sc[hh] + jnp.dot(pr, va, preferred_element_type=F32)
            m_sc[hh] = m_new

    def body(j, carry):
        step(j, False)
        return carry

    lax.fori_loop(0, i // ratio, body, 0)
    step(i // ratio, True)
    outs = []
    for hh in range(2):
        a = acc_sc[hh]
        outs.append(a / pltpu.roll(a, D_HEAD, axis=1))
    o_ref[...] = jnp.where(lane < D_HEAD, outs[0], outs[1]).astype(o_ref.dtype)


def _fox_prompt(q, kb, vb, c, ct, batch, t, tq, tk):
    nq = t // tq
    ct4 = ct.reshape(N_HEADS // 2, 2, batch * t)
    return pl.pallas_call(
        functools.partial(_fox_prompt_kernel, tq=tq, tk=tk),
        grid=(batch, N_HEADS // 2, nq),
        in_specs=[pl.BlockSpec((tq, LANES), lambda b, p, i: (b * nq + i, p)),
                  pl.BlockSpec((t, LANES), lambda b, p, i: (b, p)),
                  pl.BlockSpec((t, LANES), lambda b, p, i: (b, p)),
                  pl.BlockSpec((1, tq, N_HEADS), lambda b, p, i: (b, i, 0)),
                  pl.BlockSpec((1, 2, t), lambda b, p, i: (p, 0, b))],
        out_specs=pl.BlockSpec((tq, LANES), lambda b, p, i: (b * nq + i, p)),
        out_shape=jax.ShapeDtypeStruct((batch * t, D_ATT), BF16),
        scratch_shapes=[pltpu.VMEM((2, tq, LANES), F32), pltpu.VMEM((2, tq, LANES), F32)],
        compiler_params=_params(("parallel", "parallel", "parallel"), 32),
        name="fox_prompt",
    )(q, kb, vb, c, ct4)


def _hgrn_chunk(zq, zf, zi, lb, gw, tri, s_sc, o_sc, *, chunk, sub, n_valid):
    fg = lb + (1.0 - lb) * jax.nn.sigmoid(zf)
    kk = 1.0 - fg
    g = jnp.log(fg)
    if n_valid < chunk:
        valid = lax.broadcasted_iota(jnp.int32, (chunk, D_REC), 0) < n_valid
        kk = jnp.where(valid, kk, 0.0)
        g = jnp.where(valid, g, 0.0)
    q = _silu(zq)
    b = jnp.dot(tri, g, precision=HIGHEST, preferred_element_type=F32)
    b_end = b[chunk - 1:chunk, :]
    qhat = (q * jnp.exp(b)).astype(BF16)
    khat = (kk * jnp.exp(b_end - b)).astype(BF16)
    vb = zi.astype(BF16)
    e_end = jnp.exp(b_end)
    s_old = [s_sc[h] for h in range(N_HEADS)]
    s_old_b = [s.astype(BF16) for s in s_old]
    for i in range(chunk // sub):
        r0, r1 = i * sub, (i + 1) * sub
        rho = b[r0 + sub // 2:r0 + sub // 2 + 1, :]
        qt = (q[r0:r1] * jnp.exp(b[r0:r1] - rho)).astype(BF16)
        kt = (kk[:r1] * jnp.exp(rho - b[:r1])).astype(BF16)
        row = lax.broadcasted_iota(jnp.int32, (sub, r1), 0) + r0
        col = lax.broadcasted_iota(jnp.int32, (sub, r1), 1)
        for h in range(N_HEADS):
            hs = slice(h * D_HEAD, (h + 1) * D_HEAD)
            a = lax.dot_general(qt[:, hs], kt[:, hs], NT_DIMS, preferred_element_type=F32)
            a = jnp.where(col <= row, a, 0.0).astype(BF16)
            o = (jnp.dot(a, vb[:r1, hs], preferred_element_type=F32)
                 + jnp.dot(qhat[r0:r1, hs], s_old_b[h], preferred_element_type=F32))
            ms = jnp.mean(o * o, axis=-1, keepdims=True)
            o_sc[r0:r1, hs] = (o * lax.rsqrt(ms + RMS_EPS)) * gw
    eye = (lax.broadcasted_iota(jnp.int32, (D_HEAD, D_HEAD), 0)
           == lax.broadcasted_iota(jnp.int32, (D_HEAD, D_HEAD), 1))
    for h in range(N_HEADS):
        hs = slice(h * D_HEAD, (h + 1) * D_HEAD)
        e_col = jnp.sum(jnp.where(eye, jnp.broadcast_to(e_end[:, hs], (D_HEAD, D_HEAD)), 0.0), axis=1, keepdims=True)
        s_sc[h] = e_col * s_old[h] + lax.dot_general(khat[:, hs], vb[:, hs], TN_DIMS, preferred_element_type=F32)


def _hgrn_pair(q, kk, b, zi, bd_mean, s_old, *, chunk, sub):
    b_end = b[chunk - 1:chunk, :]
    head0 = lax.broadcasted_iota(jnp.int32, (1, LANES), 1) < D_HEAD
    qhat = (q * jnp.exp(b)).astype(BF16)
    khat = (kk * jnp.exp(b_end - b)).astype(BF16)
    vb = zi.astype(BF16)
    zero_b = jnp.zeros((), BF16)
    v0 = jnp.where(head0, vb, zero_b)
    v1 = jnp.where(head0, zero_b, vb)
    s_old_b = s_old.astype(BF16)
    a0_rows, a1_rows = [], []
    for i in range(chunk // sub):
        r0, r1 = i * sub, (i + 1) * sub
        rho = b[r0 + sub // 2:r0 + sub // 2 + 1, :]
        qt = (q[r0:r1] * jnp.exp(b[r0:r1] - rho)).astype(BF16)
        kt = (kk * jnp.exp(rho - b)).astype(BF16)
        both = jnp.concatenate([jnp.where(head0, qt, zero_b), jnp.where(head0, zero_b, qt)], axis=0)
        a = lax.dot_general(both, kt, NT_DIMS, preferred_element_type=F32)
        a0_rows.append(a[:sub])
        a1_rows.append(a[sub:])
    causal = (lax.broadcasted_iota(jnp.int32, (chunk, chunk), 1)
              <= lax.broadcasted_iota(jnp.int32, (chunk, chunk), 0))
    a0 = jnp.where(causal, jnp.concatenate(a0_rows, axis=0), 0.0).astype(BF16)
    a1 = jnp.where(causal, jnp.concatenate(a1_rows, axis=0), 0.0).astype(BF16)
    o = jnp.dot(jnp.concatenate([a0, a1, qhat], axis=1), jnp.concatenate([v0, v1, s_old_b], axis=0),
                preferred_element_type=F32)
    o2 = o * o
    hi = o2.astype(BF16)
    lo = (o2 - hi.astype(F32)).astype(BF16)
    ms = jnp.dot(jnp.concatenate([hi, lo], axis=1), jnp.concatenate([bd_mean, bd_mean], axis=0),
                 preferred_element_type=F32)
    outs = [o * lax.rsqrt(ms + RMS_EPS)]
    ri = lax.broadcasted_iota(jnp.int32, (LANES, LANES), 0)
    ci = lax.broadcasted_iota(jnp.int32, (LANES, LANES), 1)
    e_col = jnp.sum(jnp.where(ri == ci, jnp.broadcast_to(jnp.exp(b_end), (LANES, LANES)), 0.0), axis=1, keepdims=True)
    upd = lax.dot_general(khat, vb, TN_DIMS, preferred_element_type=F32)
    s_new = e_col * s_old + jnp.where((ri // D_HEAD) == (ci // D_HEAD), upd, 0.0)
    return jnp.concatenate(outs, axis=0), s_new


def _hgrn_prompt_kernel(z_ref, lb_ref, gw_ref, tri_ref, bd_ref, o_ref, s_ref, s_sc, *, chunk, sub):
    j = pl.program_id(1)

    @pl.when(j == 0)
    def _():
        s_sc[...] = jnp.zeros(s_sc.shape, F32)

    s_olds = [s_sc[p] for p in range(N_HEADS // 2)]
    z = z_ref[...]
    lb = lb_ref[...]
    fg = lb + (1.0 - lb) * jax.nn.sigmoid(z[:, D_REC:2 * D_REC])
    kk = 1.0 - fg
    q = _silu(z[:, :D_REC])
    b = jnp.dot(tri_ref[...], jnp.log(fg), precision=HIGHEST, preferred_element_type=F32)
    outs = []
    for p in range(N_HEADS // 2):
        ps = slice(p * LANES, (p + 1) * LANES)
        o, s_new = _hgrn_pair(q[:, ps], kk[:, ps], b[:, ps], z[:, 2 * D_REC + p * LANES:2 * D_REC + (p + 1) * LANES],
                              bd_ref[...], s_olds[p], chunk=chunk, sub=sub)
        gate = _silu(z[:, 3 * D_REC + p * LANES:3 * D_REC + (p + 1) * LANES])
        outs.append(((o * gw_ref[...]) * gate).astype(o_ref.dtype))
        s_olds[p] = s_new
    for p in range(N_HEADS // 2):
        s_sc[p] = s_olds[p]
    o_ref[...] = jnp.concatenate(outs, axis=1)

    @pl.when(j == pl.num_programs(1) - 1)
    def _():
        for h in range(N_HEADS):
            d0 = (h % 2) * D_HEAD
            s_ref[0, h] = s_sc[h // 2][d0:d0 + D_HEAD, d0:d0 + D_HEAD]


def _hgrn_prompt(zrec, lb, gw, batch, t, chunk, sub):
    nc = t // chunk
    idx = jnp.arange(chunk)
    tri = (idx[:, None] >= idx[None, :]).astype(F32)
    li = jnp.arange(LANES) // D_HEAD
    bd_mean = ((li[:, None] == li[None, :]).astype(F32) / D_HEAD).astype(BF16)
    gw2 = jnp.concatenate([gw, gw], axis=1)
    return pl.pallas_call(
        functools.partial(_hgrn_prompt_kernel, chunk=chunk, sub=sub),
        grid=(batch, nc),
        in_specs=[pl.BlockSpec((chunk, 4 * D_REC), lambda b, j: (b * nc + j, 0)),
                  pl.BlockSpec((1, D_REC), lambda b, j: (0, 0)),
                  pl.BlockSpec((1, LANES), lambda b, j: (0, 0)),
                  pl.BlockSpec((chunk, chunk), lambda b, j: (0, 0)),
                  pl.BlockSpec((LANES, LANES), lambda b, j: (0, 0))],
        out_specs=[pl.BlockSpec((chunk, D_REC), lambda b, j: (b * nc + j, 0)),
                   pl.BlockSpec((1, N_HEADS, D_HEAD, D_HEAD), lambda b, j: (b, 0, 0, 0))],
        out_shape=[jax.ShapeDtypeStruct((batch * t, D_REC), BF16),
                   jax.ShapeDtypeStruct((batch, N_HEADS, D_HEAD, D_HEAD), F32)],
        scratch_shapes=[pltpu.VMEM((N_HEADS // 2, LANES, LANES), F32)],
        compiler_params=_params(("parallel", "arbitrary"), 32),
        name="hgrn_prompt",
    )(zrec, lb, gw2, tri, bd_mean)


def _hgrn_sample_kernel(z_ref, s0_ref, lb_ref, gw_ref, tri_ref, o_ref, s_ref, s_sc, o_sc, *, seqs, t, chunk):
    lb = lb_ref[...]
    gw = gw_ref[...]
    tri = tri_ref[...]
    pad = jnp.zeros((chunk - t, D_REC), F32)
    for s in range(seqs):
        z = z_ref[s]
        s_sc[...] = s0_ref[s]
        zq, zf, zi = (jnp.concatenate([z[:, u * D_REC:(u + 1) * D_REC], pad], axis=0) for u in range(3))
        _hgrn_chunk(zq, zf, zi, lb, gw, tri, s_sc, o_sc, chunk=chunk, sub=chunk, n_valid=t)
        o_ref[s] = (o_sc[:t, :] * _silu(z[:, 3 * D_REC:])).astype(o_ref.dtype)
        s_ref[s] = s_sc[...]


def _hgrn_sample(zrec, s0, lb, gw, batch, t, seqs):
    chunk = 8
    idx = jnp.arange(chunk)
    tri = (idx[:, None] >= idx[None, :]).astype(F32)
    return pl.pallas_call(
        functools.partial(_hgrn_sample_kernel, seqs=seqs, t=t, chunk=chunk),
        grid=(batch // seqs,),
        in_specs=[pl.BlockSpec((seqs, t, 4 * D_REC), lambda b: (b, 0, 0)),
                  pl.BlockSpec((seqs, N_HEADS, D_HEAD, D_HEAD), lambda b: (b, 0, 0, 0)),
                  pl.BlockSpec((1, D_REC), lambda b: (0, 0)),
                  pl.BlockSpec((1, D_HEAD), lambda b: (0, 0)),
                  pl.BlockSpec((chunk, chunk), lambda b: (0, 0))],
        out_specs=[pl.BlockSpec((seqs, t, D_REC), lambda b: (b, 0, 0)),
                   pl.BlockSpec((seqs, N_HEADS, D_HEAD, D_HEAD), lambda b: (b, 0, 0, 0))],
        out_shape=[jax.ShapeDtypeStruct((batch, t, D_REC), BF16),
                   jax.ShapeDtypeStruct((batch, N_HEADS, D_HEAD, D_HEAD), F32)],
        scratch_shapes=[pltpu.VMEM((N_HEADS, D_HEAD, D_HEAD), F32), pltpu.VMEM((chunk, D_REC), F32)],
        compiler_params=_params(("parallel",), 32),
        name="hgrn_sample",
    )(zrec.reshape(batch, t, 4 * D_REC), s0, lb, gw, tri)


def _fox_sample_kernel(pt_ref, q_ref, kn_ref, vn_ref, lft_ref, gs_ref, ck_ref, cv_ref, clf_ref, o_ref,
                       kbuf, vbuf, lbuf, sem, *, t, ppb, n_pages):
    b = pl.program_id(0)
    nb = pl.num_programs(0)
    nchunk = n_pages // ppb

    def chunk_copies(bb, c, slot):
        for u in range(ppb):
            page = pt_ref[bb, n_pages - 1 - (c * ppb + u)]
            yield pltpu.make_async_copy(ck_ref.at[page], kbuf.at[slot, u], sem.at[0, slot])
            yield pltpu.make_async_copy(cv_ref.at[page], vbuf.at[slot, u], sem.at[1, slot])
            yield pltpu.make_async_copy(clf_ref.at[page], lbuf.at[slot, u], sem.at[2, slot])

    def start_chunk(bb, c, slot):
        for cp in chunk_copies(bb, c, slot):
            cp.start()

    def wait_chunk(slot):
        pltpu.make_async_copy(ck_ref.at[pl.ds(0, ppb)], kbuf.at[slot], sem.at[0, slot]).wait()
        pltpu.make_async_copy(cv_ref.at[pl.ds(0, ppb)], vbuf.at[slot], sem.at[1, slot]).wait()
        pltpu.make_async_copy(clf_ref.at[pl.ds(0, ppb)], lbuf.at[slot], sem.at[2, slot]).wait()

    @pl.when(b == 0)
    def _():
        start_chunk(0, 0, 0)

    nrow = t * N_HEADS
    lane = lax.broadcasted_iota(jnp.int32, (nrow, D_ATT), 1)
    rowi = lax.broadcasted_iota(jnp.int32, (nrow, D_ATT), 0)
    head_lanes = (lane // D_HEAD) == (rowi % N_HEADS)
    q = q_ref[0]
    qbd = jnp.concatenate([jnp.broadcast_to(q[u:u + 1, :], (N_HEADS, D_ATT)) for u in range(t)], axis=0)
    qbd = jnp.where(head_lanes, qbd, jnp.zeros_like(qbd))

    lft = lft_ref[0]
    cols = [lft[:, 0:1]]
    for u in range(1, t):
        cols.append(cols[-1] + lft[:, u:u + 1])
    cq_t = jnp.concatenate(cols, axis=1)
    cq_col = jnp.concatenate(cols, axis=0)
    zpad = jnp.zeros((N_HEADS - t, D_ATT), F32)
    kn = jnp.concatenate([kn_ref[0], zpad], axis=0).astype(BF16)
    vn = jnp.concatenate([vn_ref[0], zpad], axis=0).astype(BF16)
    cq_t8 = jnp.concatenate([cq_t, jnp.zeros((N_HEADS, N_HEADS - t), F32)], axis=1)
    s = lax.dot_general(qbd, kn, NT_DIMS, preferred_element_type=F32)
    s = s + cq_col - jnp.concatenate([cq_t8] * t, axis=0)
    r2 = lax.broadcasted_iota(jnp.int32, (nrow, N_HEADS), 0) // N_HEADS
    c2 = lax.broadcasted_iota(jnp.int32, (nrow, N_HEADS), 1)
    s = jnp.where(c2 <= r2, s, NEG_BIG)
    m = jnp.max(s, axis=-1, keepdims=True)
    p0 = jnp.exp(s - m)
    l = jnp.sum(p0, axis=-1, keepdims=True)
    acc = jnp.dot(p0.astype(BF16), vn, preferred_element_type=F32)
    carry = jnp.zeros((N_HEADS, 1), F32)

    for c in range(nchunk):
        slot = (b * nchunk + c) % 2
        if c + 1 < nchunk:
            start_chunk(b, c + 1, 1 - slot)
        else:
            @pl.when(b + 1 < nb)
            def _():
                start_chunk(b + 1, 0, 1 - slot)
        wait_chunk(slot)
        lf_pages = [lbuf[slot, u] for u in range(ppb)]
        insuf = jnp.dot(jnp.concatenate(lf_pages, axis=0), gs_ref[...], precision=HIGHEST,
                        preferred_element_type=F32)
        biases = []
        for u in range(ppb):
            biases.append(insuf[u * N_HEADS:(u + 1) * N_HEADS, :] + carry)
            carry = carry + jnp.sum(lf_pages[u], axis=-1, keepdims=True)
        bias = jnp.concatenate(biases, axis=1)
        bias = jnp.concatenate([bias] * t, axis=0) + cq_col
        kcat = jnp.concatenate([kbuf[slot, u].astype(BF16) for u in range(ppb)], axis=1)
        vcat = jnp.concatenate([vbuf[slot, u].astype(BF16) for u in range(ppb)], axis=1)
        s = jnp.dot(qbd, kcat, preferred_element_type=F32) + bias
        m_new = jnp.maximum(m, jnp.max(s, axis=-1, keepdims=True))
        alpha = jnp.exp(m - m_new)
        pr = jnp.exp(s - m_new)
        l = alpha * l + jnp.sum(pr, axis=-1, keepdims=True)
        acc = alpha * acc + lax.dot_general(pr.astype(BF16), vcat, NT_DIMS, preferred_element_type=F32)
        m = m_new

    o = jnp.where(head_lanes, acc / l, 0.0)
    o_ref[0] = jnp.sum(o.reshape(t, N_HEADS, D_ATT), axis=1).astype(o_ref.dtype)


def _fox_sample(page_table, q, kn, vn, lft, cache_k, cache_v, cache_lft, batch, t, ppb):
    n_pages = page_table.shape[1]
    ppb = min(ppb, n_pages)
    idx = jnp.arange(PAGE_SIZE)
    gs = (idx[:, None] > idx[None, :]).astype(F32)
    seq = lambda w: pl.BlockSpec((1, t, w), lambda b, pt: (b, 0, 0))
    hbm = pl.BlockSpec(memory_space=pl.ANY)
    grid_spec = pltpu.PrefetchScalarGridSpec(
        num_scalar_prefetch=1,
        grid=(batch,),
        in_specs=[seq(D_ATT), seq(D_ATT), seq(D_ATT),
                  pl.BlockSpec((1, N_HEADS, t), lambda b, pt: (b, 0, 0)),
                  pl.BlockSpec((PAGE_SIZE, PAGE_SIZE), lambda b, pt: (0, 0)),
                  hbm, hbm, hbm],
        out_specs=pl.BlockSpec((1, t, D_ATT), lambda b, pt: (b, 0, 0)),
        scratch_shapes=[pltpu.VMEM((2, ppb, D_ATT, PAGE_SIZE), F32), pltpu.VMEM((2, ppb, D_ATT, PAGE_SIZE), F32),
                        pltpu.VMEM((2, ppb, N_HEADS, PAGE_SIZE), F32), pltpu.SemaphoreType.DMA((3, 2))],
    )
    return pl.pallas_call(
        functools.partial(_fox_sample_kernel, t=t, ppb=ppb, n_pages=n_pages),
        grid_spec=grid_spec,
        out_shape=jax.ShapeDtypeStruct((batch, t, D_ATT), BF16),
        compiler_params=_params(("arbitrary",), 48),
        name="fox_sample",
    )(page_table, q.reshape(batch, t, D_ATT), kn.reshape(batch, t, D_ATT), vn.reshape(batch, t, D_ATT), lft, gs,
      cache_k, cache_v, cache_lft)


def _outproj_router_kernel(x_ref, orec_ref, oatt_ref, wo_ref, nw_ref, wr_ref, br_ref, cnt0_ref, tril_ref,
                           h_ref, xn_ref, idx_ref, gate_ref, rank_ref, cnt_ref, base_sc):
    i = pl.program_id(0)

    @pl.when(i == 0)
    def _():
        base_sc[...] = cnt0_ref[...]

    tm = x_ref.shape[0]
    h = (x_ref[...]
         + jnp.dot(orec_ref[...], wo_ref[:D_REC, :], preferred_element_type=F32)
         + jnp.dot(oatt_ref[...], wo_ref[D_REC:, :], preferred_element_type=F32))
    h_ref[...] = h
    r = lax.rsqrt(jnp.mean(h * h, axis=-1, keepdims=True) + RMS_EPS)
    xn = (h * r) * nw_ref[...]
    xn_ref[...] = xn
    x_hi = xn.astype(BF16)
    x_lo = (xn - x_hi.astype(F32)).astype(BF16)
    logits = (jnp.dot(x_hi, wr_ref[0], preferred_element_type=F32)
              + (jnp.dot(x_lo, wr_ref[0], preferred_element_type=F32)
                 + jnp.dot(x_hi, wr_ref[1], preferred_element_type=F32))) + br_ref[...]
    lane = lax.broadcasted_iota(jnp.int32, (tm, LANES), 1)
    vals, hots = [], []
    idx_out = jnp.zeros((tm, LANES), jnp.int32)
    for k in range(TOP_K):
        m = jnp.max(logits, axis=-1, keepdims=True)
        sel = jnp.min(jnp.where(logits == m, lane, LANES), axis=-1, keepdims=True)
        hot = lane == sel
        vals.append(m)
        hots.append(hot)
        idx_out = jnp.where(lane == k, sel, idx_out)
        logits = jnp.where(hot, -jnp.inf, logits)
    idx_ref[...] = idx_out
    es = [jnp.exp(v - vals[0]) for v in vals]
    denom = es[0] + es[1] + es[2] + es[3]
    gate_out = jnp.zeros((tm, LANES), F32)
    for k in range(TOP_K):
        gate_out = jnp.where(lane == k, es[k] / denom, gate_out)
    gate_ref[...] = gate_out
    base = base_sc[...]
    rank_out = jnp.zeros((tm, LANES), jnp.int32)
    for k in range(TOP_K):
        hot_f = hots[k].astype(F32)
        before = jnp.dot(tril_ref[...], hot_f.astype(BF16), preferred_element_type=F32) + base
        rank = jnp.sum(jnp.where(hots[k], before, 0.0), axis=-1, keepdims=True)
        rank_out = jnp.where(lane == k, rank.astype(jnp.int32), rank_out)
        base = base + jnp.sum(hot_f, axis=0, keepdims=True)
    rank_ref[...] = rank_out
    base_sc[...] = base
    cnt_ref[...] = base


def _outproj_router(x, orec, oatt, wo, nw, wr, br, cnt0, tm):
    n = x.shape[0]
    idx = jnp.arange(tm)
    tril = (idx[:, None] > idx[None, :]).astype(BF16)
    full = lambda a: pl.BlockSpec(a.shape, lambda i: (0,) * a.ndim)
    row = lambda w: pl.BlockSpec((tm, w), lambda i: (i, 0))
    return pl.pallas_call(
        _outproj_router_kernel,
        grid=(n // tm,),
        in_specs=[row(D_MODEL), row(D_REC), row(D_ATT), full(wo), full(nw), full(wr), full(br), full(cnt0), full(tril)],
        out_specs=[row(D_MODEL), row(D_MODEL), row(LANES), row(LANES), row(LANES),
                   pl.BlockSpec((1, LANES), lambda i: (0, 0))],
        out_shape=[jax.ShapeDtypeStruct((n, D_MODEL), F32), jax.ShapeDtypeStruct((n, D_MODEL), F32),
                   jax.ShapeDtypeStruct((n, LANES), jnp.int32), jax.ShapeDtypeStruct((n, LANES), F32),
                   jax.ShapeDtypeStruct((n, LANES), jnp.int32), jax.ShapeDtypeStruct((1, LANES), F32)],
        scratch_shapes=[pltpu.VMEM((1, LANES), F32)],
        compiler_params=_params(("arbitrary",), 40),
        name="outproj_router",
    )(x, orec, oatt, wo, nw, wr, br, cnt0, tril)


def _dispatch_kernel(gend_ref, gsize_ref, total_ref, dest_ref, xp_ref, xs_ref, out_ref, zbuf, sem, zsem,
                     *, n_prompt_tiles, tile_rows):
    i = pl.program_id(0)
    n_tiles = out_ref.shape[0] // tile_rows

    @pl.when(i == 0)
    def _():
        zbuf[...] = jnp.zeros(zbuf.shape, zbuf.dtype)

        def zero_tile(start):
            cp = pltpu.make_async_copy(zbuf, out_ref.at[pl.ds(pl.multiple_of(start, tile_rows), tile_rows), :], zsem)
            cp.start()
            cp.wait()

        for e in range(N_EXPERTS):
            @pl.when(gsize_ref[e] > 0)
            def _():
                zero_tile(gend_ref[e] - tile_rows)

        def tail(tt, carry):
            zero_tile(tt * tile_rows)
            return carry

        lax.fori_loop(total_ref[0] // tile_rows, n_tiles, tail, 0)

    def scatter(src_ref):
        n = src_ref.shape[0]

        def issue(g, carry):
            r0 = pl.multiple_of(g * SUBLANES, SUBLANES)
            for u in range(SUBLANES):
                for k in range(TOP_K):
                    pltpu.make_async_copy(src_ref.at[pl.ds(r0 + u, 1), :],
                                          out_ref.at[pl.ds(dest_ref[0, 0, (r0 + u) * TOP_K + k], 1), :],
                                          sem).start(priority=k % 2)
            return carry

        lax.fori_loop(0, n // SUBLANES, issue, 0)
        for k in range(TOP_K):
            pltpu.make_async_copy(src_ref, out_ref.at[pl.ds(0, n), :], sem).wait()

    @pl.when(i < n_prompt_tiles)
    def _():
        scatter(xp_ref)

    @pl.when(i == n_prompt_tiles)
    def _():
        scatter(xs_ref)


def _dispatch(group_end, group_size, dest_p, dest_s, xn_p, xn_s, n_rows, tm, tile_rows):
    n_p, n_s = xn_p.shape[0], xn_s.shape[0]
    npt = n_p // tm
    dest = jnp.concatenate([dest_p.reshape(npt, tm * TOP_K),
                            jnp.pad(dest_s, (0, (tm - n_s) * TOP_K)).reshape(1, tm * TOP_K)], axis=0)
    grid_spec = pltpu.PrefetchScalarGridSpec(
        num_scalar_prefetch=3,
        grid=(npt + 1,),
        in_specs=[pl.BlockSpec((1, 1, tm * TOP_K), lambda i, *_: (i, 0, 0), memory_space=pltpu.SMEM),
                  pl.BlockSpec((tm, D_MODEL), lambda i, *_: (jnp.minimum(i, npt - 1), 0)),
                  pl.BlockSpec((n_s, D_MODEL), lambda i, *_: (0, 0))],
        out_specs=pl.BlockSpec(memory_space=pl.ANY),
        scratch_shapes=[pltpu.VMEM((tile_rows, D_MODEL), F32), pltpu.SemaphoreType.DMA(()),
                        pltpu.SemaphoreType.DMA(())],
    )
    return pl.pallas_call(
        functools.partial(_dispatch_kernel, n_prompt_tiles=npt, tile_rows=tile_rows),
        grid_spec=grid_spec,
        out_shape=jax.ShapeDtypeStruct((n_rows, D_MODEL), F32),
        compiler_params=_params(("arbitrary",), 32),
        name="moe_dispatch",
    )(group_end, group_size, group_end[-1:], dest.reshape(npt + 1, 1, tm * TOP_K), xn_p, xn_s)


def _expert_mlp_kernel(te_ref, tv_ref, tf_ref, x_ref, wg_ref, bg_ref, wu_ref, bu_ref, wd_ref, bd_ref, y_ref,
                       wg_sc, wu_sc, wd_sc):
    i = pl.program_id(0)
    del te_ref

    @pl.when(tf_ref[i] != 0)
    def _():
        wg_sc[...] = wg_ref[0].astype(BF16)
        wu_sc[...] = wu_ref[0].astype(BF16)
        wd_sc[...] = wd_ref[0].astype(BF16)

    @pl.when(tv_ref[i] != 0)
    def _():
        x = x_ref[...].astype(BF16)
        gate = jnp.minimum(jnp.dot(x, wg_sc[...], preferred_element_type=F32) + bg_ref[0], SWIGLU_LIMIT)
        up = jnp.clip(jnp.dot(x, wu_sc[...], preferred_element_type=F32) + bu_ref[0], -SWIGLU_LIMIT, SWIGLU_LIMIT)
        glu = gate * jax.nn.sigmoid(gate * SWIGLU_ALPHA)
        mid = ((up + 1.0) * glu).astype(BF16)
        y_ref[...] = jnp.dot(mid, wd_sc[...], preferred_element_type=F32) + bd_ref[0]

    @pl.when(tv_ref[i] == 0)
    def _():
        y_ref[...] = jnp.zeros(y_ref.shape, y_ref.dtype)


def _expert_mlp(tile_expert, tile_valid, tile_first, xs, wg, bg, wu, bu, wd, bd, tm):
    p = xs.shape[0]
    wspec = lambda: pl.BlockSpec((1, D_MODEL, D_FF), lambda i, te, tv, tf: (te[i], 0, 0))
    bspec = lambda: pl.BlockSpec((1, 1, D_FF), lambda i, te, tv, tf: (te[i], 0, 0))
    grid_spec = pltpu.PrefetchScalarGridSpec(
        num_scalar_prefetch=3,
        grid=(p // tm,),
        in_specs=[pl.BlockSpec((tm, D_MODEL), lambda i, te, tv, tf: (i * tv[i], 0)),
                  wspec(), bspec(), wspec(), bspec(), wspec(), bspec()],
        out_specs=pl.BlockSpec((tm, D_MODEL), lambda i, te, tv, tf: (i, 0)),
        scratch_shapes=[pltpu.VMEM((D_MODEL, D_FF), BF16), pltpu.VMEM((D_MODEL, D_FF), BF16),
                        pltpu.VMEM((D_FF, D_MODEL), BF16)],
    )
    return pl.pallas_call(
        _expert_mlp_kernel,
        grid_spec=grid_spec,
        out_shape=jax.ShapeDtypeStruct((p, D_MODEL), F32),
        compiler_params=_params(("arbitrary",), 56),
        name="expert_mlp",
    )(tile_expert, tile_valid, tile_first, xs, wg, bg, wu, bu, wd, bd)


def _combine_kernel(dest_ref, gate_ref, h_ref, nw_ref, ys_ref, y_ref, buf, sem):
    tm = h_ref.shape[0]

    def issue(g, carry):
        r0 = pl.multiple_of(g * SUBLANES, SUBLANES)
        for u in range(SUBLANES):
            for k in range(TOP_K):
                pltpu.make_async_copy(ys_ref.at[pl.ds(dest_ref[0, 0, (r0 + u) * TOP_K + k], 1), :],
                                      buf.at[k, pl.ds(r0 + u, 1), :], sem).start(priority=k % 2)
        return carry

    lax.fori_loop(0, tm // SUBLANES, issue, 0)
    for k in range(TOP_K):
        pltpu.make_async_copy(ys_ref.at[pl.ds(0, tm), :], buf.at[k], sem).wait()
    gates = gate_ref[...]
    moe = gates[:, 0:1] * buf[0]
    for k in range(1, TOP_K):
        moe = moe + gates[:, k:k + 1] * buf[k]
    h = h_ref[...] + moe
    r = lax.rsqrt(jnp.mean(h * h, axis=-1, keepdims=True) + RMS_EPS)
    y_ref[...] = (h * r) * nw_ref[...]


def _combine(dest, gates, h, nw, ys, tm):
    n = h.shape[0]
    return pl.pallas_call(
        _combine_kernel,
        grid=(n // tm,),
        in_specs=[pl.BlockSpec((1, 1, tm * TOP_K), lambda i: (i, 0, 0), memory_space=pltpu.SMEM),
                  pl.BlockSpec((tm, LANES), lambda i: (i, 0)),
                  pl.BlockSpec((tm, D_MODEL), lambda i: (i, 0)),
                  pl.BlockSpec((1, D_MODEL), lambda i: (0, 0)),
                  pl.BlockSpec(memory_space=pl.ANY)],
        out_specs=pl.BlockSpec((tm, D_MODEL), lambda i: (i, 0)),
        out_shape=jax.ShapeDtypeStruct((n, D_MODEL), F32),
        scratch_shapes=[pltpu.VMEM((TOP_K, tm, D_MODEL), F32), pltpu.SemaphoreType.DMA(())],
        compiler_params=_params(("arbitrary",), 32),
        name="moe_combine",
    )(dest.reshape(n // tm, 1, tm * TOP_K), gates, h, nw, ys)


TM_TOKENS = 512
TM_EXPERT = 512
TM_COMBINE = 256
TQ_PROMPT = 512
TK_PROMPT = 512
HGRN_CHUNK = 128
HGRN_SUB = 32
PAGES_PER_STEP = 16
HGRN_SAMPLE_SEQS = 8


def _step(x_prompt, x_sample, cache_k, cache_v, cache_logf, state_hgrn, page_table,
          norm_mix_w, w_in, b_fa, lb_param, gnorm_w, w_out, norm_ffn_w,
          w_router, b_router, w_gate, b_gate, w_up, b_up, w_down, b_down, norm_final_w):
    batch, t, _ = x_prompt.shape
    dec_batch, dec_t, _ = x_sample.shape
    n_p, n_s = batch * t, dec_batch * dec_t
    n_pool = cache_k.shape[1]

    w_in_b = w_in[0].astype(BF16)
    wrec = w_in_b[:, :4 * D_REC]
    watt = w_in_b[:, 4 * D_REC:4 * D_REC + 3 * D_ATT]
    wfa = jnp.pad(w_in_b[:, 4 * D_REC + 3 * D_ATT:], ((0, 0), (0, LANES - N_HEADS)))
    wfat = jnp.pad(w_in_b[:, 4 * D_REC + 3 * D_ATT:].T, ((0, 16 - N_HEADS), (0, 0)))
    bfa = b_fa[0].reshape(1, N_HEADS)
    bfat = b_fa[0].reshape(N_HEADS, 1)
    nw_mix = norm_mix_w[0].reshape(1, D_MODEL)
    nw_ffn = norm_ffn_w[0].reshape(1, D_MODEL)
    nw_fin = norm_final_w.reshape(1, D_MODEL)
    lb = jnp.cumsum(jax.nn.softmax(lb_param.astype(F32), axis=0), axis=0)[0].reshape(1, D_REC)
    gw = gnorm_w.reshape(1, D_HEAD)
    wo = w_out[0].astype(BF16)
    wr_f = jnp.pad(w_router[0], ((0, 0), (0, LANES - N_EXPERTS)))
    wr_hi = wr_f.astype(BF16)
    wr = jnp.stack([wr_hi, (wr_f - wr_hi.astype(F32)).astype(BF16)])
    br = jnp.pad(b_router[0], (0, LANES - N_EXPERTS), constant_values=NEG_BIG).reshape(1, LANES)
    wg, wu, wd = w_gate[0], w_up[0], w_down[0]
    bg, bu, bd = (b[0].reshape(N_EXPERTS, 1, -1) for b in (b_gate, b_up, b_down))

    xp = x_prompt.reshape(n_p, D_MODEL)
    xs = x_sample.reshape(n_s, D_MODEL)
    tm_p, tm_s = min(TM_TOKENS, n_p), min(TM_TOKENS, n_s)
    proj = functools.partial(_inproj, nw=nw_mix, wrec=wrec, watt=watt, wfa=wfa, wfat=wfat, bfa=bfa, bfat=bfat)
    zrec_p, q_p, kt_p, vt_p, kb_p, vb_p, lf_p, lft_p = proj(xp, tm=tm_p, seq_len=t)
    zrec_s, q_s, k_s, v_s, _, _, lf_s, lft_s = proj(xs, tm=tm_s)

    c_p, ct_p = _cumsum_logf(lf_p, lft_p, batch, t)
    oatt_p = _fox_prompt(q_p, kb_p, vb_p, c_p, ct_p, batch, t, TQ_PROMPT, TK_PROMPT)
    orec_p, state_p = _hgrn_prompt(zrec_p, lb, gw, batch, t, HGRN_CHUNK, HGRN_SUB)

    orec_s, state_s = _hgrn_sample(zrec_s, state_hgrn[0], lb, gw, dec_batch, dec_t, HGRN_SAMPLE_SEQS)
    lft_s3 = lft_s.reshape(N_HEADS, dec_batch, dec_t).transpose(1, 0, 2)
    cache_lft = cache_logf[0].transpose(0, 2, 1)
    oatt_s = _fox_sample(page_table, q_s, k_s, v_s, lft_s3,
                         cache_k[0].transpose(0, 2, 3, 1).reshape(n_pool, D_ATT, PAGE_SIZE),
                         cache_v[0].transpose(0, 2, 3, 1).reshape(n_pool, D_ATT, PAGE_SIZE),
                         cache_lft, dec_batch, dec_t, PAGES_PER_STEP)

    route = functools.partial(_outproj_router, wo=wo, nw=nw_ffn, wr=wr, br=br)
    h_p, xn_p, idx_p, gate_p, rank_p, cnt_p = route(xp, orec_p, oatt_p, cnt0=jnp.zeros((1, LANES), F32), tm=tm_p)
    h_s, xn_s, idx_s, gate_s, rank_s, cnt = route(xs, orec_s.reshape(n_s, D_REC), oatt_s.reshape(n_s, D_ATT),
                                                  cnt0=cnt_p, tm=tm_s)

    counts = cnt[0, :N_EXPERTS].astype(jnp.int32)
    padded = ((counts + TM_EXPERT - 1) // TM_EXPERT) * TM_EXPERT
    group_end = jnp.cumsum(padded)
    group_start = group_end - padded
    n_rows = (n_p + n_s) * TOP_K + N_EXPERTS * TM_EXPERT
    tile_start = jnp.arange(n_rows // TM_EXPERT, dtype=jnp.int32) * TM_EXPERT
    tile_expert = jnp.minimum(jnp.sum((tile_start[:, None] >= group_end[None, :]).astype(jnp.int32), axis=1),
                              N_EXPERTS - 1)
    tile_valid = (tile_start < group_end[-1]).astype(jnp.int32)
    tile_first = jnp.concatenate([jnp.ones((1,), jnp.int32),
                                  (tile_expert[1:] != tile_expert[:-1]).astype(jnp.int32)])
    dest_p = (group_start[idx_p[:, :TOP_K]] + rank_p[:, :TOP_K]).reshape(-1)
    dest_s = (group_start[idx_s[:, :TOP_K]] + rank_s[:, :TOP_K]).reshape(-1)

    x_sorted = _dispatch(group_end.astype(jnp.int32), padded, dest_p, dest_s, xn_p, xn_s, n_rows, tm_p, TM_EXPERT)
    y_sorted = _expert_mlp(tile_expert, tile_valid, tile_first, x_sorted, wg, bg, wu, bu, wd, bd, TM_EXPERT)
    y_p = _combine(dest_p, gate_p, h_p, nw_fin, y_sorted, min(TM_COMBINE, n_p))
    y_s = _combine(dest_s, gate_s, h_s, nw_fin, y_sorted, min(TM_COMBINE, n_s))

    heads = lambda a, b_, t_: a.reshape(1, b_, t_, N_HEADS, D_HEAD)
    heads_t = lambda a: a.reshape(1, batch, N_HEADS, D_HEAD, t).transpose(0, 1, 4, 2, 3)
    return (y_p.reshape(batch, t, D_MODEL), y_s.reshape(dec_batch, dec_t, D_MODEL),
            heads_t(kt_p), heads_t(vt_p), lf_p.reshape(1, batch, t, N_HEADS), state_p[None],
            heads(k_s, dec_batch, dec_t), heads(v_s, dec_batch, dec_t), lf_s.reshape(1, dec_batch, dec_t, N_HEADS),
            state_s[None])


def kernel(x_prompt, x_sample, cache_k, cache_v, cache_logf, state_hgrn, page_table, norm_mix_w, w_in, b_fa, lb_param, gnorm_w, w_out, norm_ffn_w, w_router, b_router, w_gate, b_gate, w_up, b_up, w_down, b_down, norm_final_w):
    return _step(x_prompt, x_sample, cache_k, cache_v, cache_logf, state_hgrn, page_table,
                 norm_mix_w, w_in, b_fa, lb_param, gnorm_w, w_out, norm_ffn_w,
                 w_router, b_router, w_gate, b_gate, w_up, b_up, w_down, b_down, norm_final_w)
```

```python
import functools

import jax
import jax.numpy as jnp
from jax import lax
from jax.experimental import pallas as pl
from jax.experimental.pallas import tpu as pltpu

D_MODEL = 1024
D_REC = 512
D_ATT = 512
N_HEADS = 8
D_HEAD = 64
N_EXPERTS = 32
TOP_K = 4
D_FF = 1024
PAGE_SIZE = 128
SWIGLU_LIMIT = 7.0
SWIGLU_ALPHA = 1.702
RMS_EPS = 1e-5

LANES = 128
SUBLANES = 8
NEG_BIG = -1e30
MIB = 1024 * 1024

F32 = jnp.float32
BF16 = jnp.bfloat16
HIGHEST = lax.Precision.HIGHEST

NT_DIMS = (((1,), (1,)), ((), ()))
TN_DIMS = (((0,), (0,)), ((), ()))


def _log_sigmoid(x):
    return jnp.minimum(x, 0.0) - jnp.log1p(jnp.exp(-jnp.abs(x)))


def _silu(x):
    return x * jax.nn.sigmoid(x)


def _params(sem, vmem_mib):
    return pltpu.CompilerParams(dimension_semantics=sem, vmem_limit_bytes=vmem_mib * MIB)


def _inproj_kernel(x_ref, nw_ref, wrec_ref, watt_ref, wfa_ref, wfat_ref, bfa_ref, bfat_ref,
                   zrec_ref, q_ref, k_ref, v_ref, kb_ref, vb_ref, lf_ref, lft_ref, *, kv_transposed):
    x = x_ref[...]
    r = lax.rsqrt(jnp.mean(x * x, axis=-1, keepdims=True) + RMS_EPS)
    xn = ((x * r) * nw_ref[...]).astype(BF16)
    zrec_ref[...] = jnp.dot(xn, wrec_ref[...], preferred_element_type=F32)
    za = jnp.dot(xn, watt_ref[...], preferred_element_type=F32)
    q_ref[...] = (za[:, :D_ATT] * (D_HEAD ** -0.5)).astype(BF16)
    k = za[:, D_ATT:2 * D_ATT]
    v = za[:, 2 * D_ATT:]
    if kv_transposed:
        k_ref[0] = k.T
        v_ref[0] = v.T
    else:
        k_ref[...] = k
        v_ref[...] = v
    kb_ref[...] = k.astype(BF16)
    vb_ref[...] = v.astype(BF16)
    fa = jnp.dot(xn, wfa_ref[...], preferred_element_type=F32)
    lf_ref[...] = _log_sigmoid(fa[:, :N_HEADS] + bfa_ref[...])
    fat = lax.dot_general(wfat_ref[...], xn, NT_DIMS, preferred_element_type=F32)
    lft_ref[...] = _log_sigmoid(fat[:N_HEADS, :] + bfat_ref[...])


def _inproj(x, nw, wrec, watt, wfa, wfat, bfa, bfat, tm, seq_len=None):
    n = x.shape[0]
    full = lambda a: pl.BlockSpec(a.shape, lambda i: (0,) * a.ndim)
    row = lambda w: pl.BlockSpec((tm, w), lambda i: (i, 0))
    if seq_len is None:
        kv_spec, kv_shape = row(D_ATT), jax.ShapeDtypeStruct((n, D_ATT), F32)
    else:
        nt = seq_len // tm
        kv_spec = pl.BlockSpec((1, D_ATT, tm), lambda i: (i // nt, 0, i % nt))
        kv_shape = jax.ShapeDtypeStruct((n // seq_len, D_ATT, seq_len), F32)
    return pl.pallas_call(
        functools.partial(_inproj_kernel, kv_transposed=seq_len is not None),
        grid=(n // tm,),
        in_specs=[row(D_MODEL), full(nw), full(wrec), full(watt), full(wfa), full(wfat), full(bfa), full(bfat)],
        out_specs=[row(4 * D_REC), row(D_ATT), kv_spec, kv_spec, row(D_ATT), row(D_ATT), row(N_HEADS),
                   pl.BlockSpec((N_HEADS, tm), lambda i: (0, i))],
        out_shape=[jax.ShapeDtypeStruct((n, 4 * D_REC), F32), jax.ShapeDtypeStruct((n, D_ATT), BF16),
                   kv_shape, kv_shape,
                   jax.ShapeDtypeStruct((n, D_ATT), BF16), jax.ShapeDtypeStruct((n, D_ATT), BF16),
                   jax.ShapeDtypeStruct((n, N_HEADS), F32), jax.ShapeDtypeStruct((N_HEADS, n), F32)],
        compiler_params=_params(("parallel",), 52),
        name="inproj",
    )(x, nw, wrec, watt, wfa, wfat, bfa, bfat)


def _cumsum_kernel(lf_ref, lft_ref, tri_ref, triu_ref, c_ref, ct_ref, *, tb):
    t = lf_ref.shape[1]
    tri = tri_ref[...]
    triu = triu_ref[...]
    carry = jnp.zeros((1, N_HEADS), F32)
    carry_t = jnp.zeros((N_HEADS, 1), F32)
    for j in range(t // tb):
        sl = slice(j * tb, (j + 1) * tb)
        cb = jnp.dot(tri, lf_ref[0, sl, :], precision=HIGHEST, preferred_element_type=F32) + carry
        c_ref[0, sl, :] = cb
        carry = cb[tb - 1:tb, :]
        cbt = jnp.dot(lft_ref[:, sl], triu, precision=HIGHEST, preferred_element_type=F32) + carry_t
        ct_ref[:, sl] = cbt
        carry_t = cbt[:, tb - 1:tb]


def _cumsum_logf(lf, lft, batch, t):
    tb = min(t, 512)
    idx = jnp.arange(tb)
    tri = (idx[:, None] >= idx[None, :]).astype(F32)
    return pl.pallas_call(
        functools.partial(_cumsum_kernel, tb=tb),
        grid=(batch,),
        in_specs=[pl.BlockSpec((1, t, N_HEADS), lambda b: (b, 0, 0)),
                  pl.BlockSpec((N_HEADS, t), lambda b: (0, b)),
                  pl.BlockSpec((tb, tb), lambda b: (0, 0)),
                  pl.BlockSpec((tb, tb), lambda b: (0, 0))],
        out_specs=[pl.BlockSpec((1, t, N_HEADS), lambda b: (b, 0, 0)),
                   pl.BlockSpec((N_HEADS, t), lambda b: (0, b))],
        out_shape=[jax.ShapeDtypeStruct((batch, t, N_HEADS), F32),
                   jax.ShapeDtypeStruct((N_HEADS, batch * t), F32)],
        compiler_params=_params(("parallel",), 32),
        name="cumsum_logf",
    )(lf.reshape(batch, t, N_HEADS), lft, tri, tri.T)


def _fox_prompt_kernel(q_ref, k_ref, v_ref, c_ref, ct_ref, o_ref, m_sc, acc_sc, *, tq, tk):
    p = pl.program_id(1)
    i = pl.program_id(2)
    ratio = tk // tq
    nrep = tk // LANES
    q = q_ref[...]
    lane = lax.broadcasted_iota(jnp.int32, (tq, LANES), 1)
    lane8 = lax.broadcasted_iota(jnp.int32, (tq, N_HEADS), 1)
    lane_k = lax.broadcasted_iota(jnp.int32, (tk, LANES), 1)
    cblk = c_ref[0]
    qs, cqs = [], []
    for hh in range(2):
        qs.append(jnp.where((lane // D_HEAD) == hh, q, jnp.zeros_like(q)))
        cq = jnp.sum(jnp.where(lane8 == 2 * p + hh, cblk, 0.0), axis=-1, keepdims=True)
        cqs.append(jnp.concatenate([jnp.broadcast_to(cq, (tq, LANES))] * nrep, axis=1))
    m_sc[...] = jnp.full(m_sc.shape, NEG_BIG, F32)
    acc_sc[...] = jnp.zeros(acc_sc.shape, F32)

    def step(j, masked):
        off = pl.multiple_of(j * tk, tk)
        kb = k_ref[pl.ds(off, tk), :]
        vb = v_ref[pl.ds(off, tk), :]
        for hh in range(2):
            va = jnp.where((lane_k // D_HEAD) == hh, vb, jnp.ones_like(vb))
            s = lax.dot_general(qs[hh], kb, NT_DIMS, preferred_element_type=F32)
            s = (s + cqs[hh]) - ct_ref[0, hh:hh + 1, pl.ds(off, tk)]
            if masked:
                row = lax.broadcasted_iota(jnp.int32, (tq, tk), 0) + (i % ratio) * tq
                col = lax.broadcasted_iota(jnp.int32, (tq, tk), 1)
                s = jnp.where(col <= row, s, NEG_BIG)
            m_prev = m_sc[hh]
            m_new = jnp.maximum(m_prev, jnp.max(s, axis=-1, keepdims=True))
            alpha = jnp.exp(m_prev - m_new)
            pr = jnp.exp(s - jnp.concatenate([m_new] * nrep, axis=1)).astype(BF16)
            acc_sc[hh] = alpha * acc_sc[hh] + jnp.dot(pr, va, preferred_element_type=F32)
            m_sc[hh] = m_new

    def body(j, carry):
        step(j, False)
        return carry

    lax.fori_loop(0, i // ratio, body, 0)
    step(i // ratio, True)
    outs = []
    for hh in range(2):
        a = acc_sc[hh]
        outs.append(a / pltpu.roll(a, D_HEAD, axis=1))
    o_ref[...] = jnp.where(lane < D_HEAD, outs[0], outs[1]).astype(o_ref.dtype)


def _fox_prompt(q, kb, vb, c, ct, batch, t, tq, tk):
    nq = t // tq
    ct4 = ct.reshape(N_HEADS // 2, 2, batch * t)
    return pl.pallas_call(
        functools.partial(_fox_prompt_kernel, tq=tq, tk=tk),
        grid=(batch, N_HEADS // 2, nq),
        in_specs=[pl.BlockSpec((tq, LANES), lambda b, p, i: (b * nq + i, p)),
                  pl.BlockSpec((t, LANES), lambda b, p, i: (b, p)),
                  pl.BlockSpec((t, LANES), lambda b, p, i: (b, p)),
                  pl.BlockSpec((1, tq, N_HEADS), lambda b, p, i: (b, i, 0)),
                  pl.BlockSpec((1, 2, t), lambda b, p, i: (p, 0, b))],
        out_specs=pl.BlockSpec((tq, LANES), lambda b, p, i: (b * nq + i, p)),
        out_shape=jax.ShapeDtypeStruct((batch * t, D_ATT), BF16),
        scratch_shapes=[pltpu.VMEM((2, tq, LANES), F32), pltpu.VMEM((2, tq, LANES), F32)],
        compiler_params=_params(("parallel", "parallel", "parallel"), 32),
        name="fox_prompt",
    )(q, kb, vb, c, ct4)


def _hgrn_chunk(zq, zf, zi, lb, gw, tri, s_sc, o_sc, *, chunk, sub, n_valid):
    fg = lb + (1.0 - lb) * jax.nn.sigmoid(zf)
    kk = 1.0 - fg
    g = jnp.log(fg)
    if n_valid < chunk:
        valid = lax.broadcasted_iota(jnp.int32, (chunk, D_REC), 0) < n_valid
        kk = jnp.where(valid, kk, 0.0)
        g = jnp.where(valid, g, 0.0)
    q = _silu(zq)
    b = jnp.dot(tri, g, precision=HIGHEST, preferred_element_type=F32)
    b_end = b[chunk - 1:chunk, :]
    qhat = (q * jnp.exp(b)).astype(BF16)
    khat = (kk * jnp.exp(b_end - b)).astype(BF16)
    vb = zi.astype(BF16)
    e_end = jnp.exp(b_end)
    s_old = [s_sc[h] for h in range(N_HEADS)]
    s_old_b = [s.astype(BF16) for s in s_old]
    for i in range(chunk // sub):
        r0, r1 = i * sub, (i + 1) * sub
        rho = b[r0 + sub // 2:r0 + sub // 2 + 1, :]
        qt = (q[r0:r1] * jnp.exp(b[r0:r1] - rho)).astype(BF16)
        kt = (kk[:r1] * jnp.exp(rho - b[:r1])).astype(BF16)
        row = lax.broadcasted_iota(jnp.int32, (sub, r1), 0) + r0
        col = lax.broadcasted_iota(jnp.int32, (sub, r1), 1)
        for h in range(N_HEADS):
            hs = slice(h * D_HEAD, (h + 1) * D_HEAD)
            a = lax.dot_general(qt[:, hs], kt[:, hs], NT_DIMS, preferred_element_type=F32)
            a = jnp.where(col <= row, a, 0.0).astype(BF16)
            o = (jnp.dot(a, vb[:r1, hs], preferred_element_type=F32)
                 + jnp.dot(qhat[r0:r1, hs], s_old_b[h], preferred_element_type=F32))
            ms = jnp.mean(o * o, axis=-1, keepdims=True)
            o_sc[r0:r1, hs] = (o * lax.rsqrt(ms + RMS_EPS)) * gw
    eye = (lax.broadcasted_iota(jnp.int32, (D_HEAD, D_HEAD), 0)
           == lax.broadcasted_iota(jnp.int32, (D_HEAD, D_HEAD), 1))
    for h in range(N_HEADS):
        hs = slice(h * D_HEAD, (h + 1) * D_HEAD)
        e_col = jnp.sum(jnp.where(eye, jnp.broadcast_to(e_end[:, hs], (D_HEAD, D_HEAD)), 0.0), axis=1, keepdims=True)
        s_sc[h] = e_col * s_old[h] + lax.dot_general(khat[:, hs], vb[:, hs], TN_DIMS, preferred_element_type=F32)


def _hgrn_pair(q, kk, b, zi, bd_mean, s_old, *, chunk, sub):
    b_end = b[chunk - 1:chunk, :]
    head0 = lax.broadcasted_iota(jnp.int32, (1, LANES), 1) < D_HEAD
    qhat = (q * jnp.exp(b)).astype(BF16)
    khat = (kk * jnp.exp(b_end - b)).astype(BF16)
    vb = zi.astype(BF16)
    zero_b = jnp.zeros((), BF16)
    v0 = jnp.where(head0, vb, zero_b)
    v1 = jnp.where(head0, zero_b, vb)
    s_old_b = s_old.astype(BF16)
    a0_rows, a1_rows = [], []
    for i in range(chunk // sub):
        r0, r1 = i * sub, (i + 1) * sub
        rho = b[r0 + sub // 2:r0 + sub // 2 + 1, :]
        qt = (q[r0:r1] * jnp.exp(b[r0:r1] - rho)).astype(BF16)
        kt = (kk * jnp.exp(rho - b)).astype(BF16)
        both = jnp.concatenate([jnp.where(head0, qt, zero_b), jnp.where(head0, zero_b, qt)], axis=0)
        a = lax.dot_general(both, kt, NT_DIMS, preferred_element_type=F32)
        a0_rows.append(a[:sub])
        a1_rows.append(a[sub:])
    causal = (lax.broadcasted_iota(jnp.int32, (chunk, chunk), 1)
              <= lax.broadcasted_iota(jnp.int32, (chunk, chunk), 0))
    a0 = jnp.where(causal, jnp.concatenate(a0_rows, axis=0), 0.0).astype(BF16)
    a1 = jnp.where(causal, jnp.concatenate(a1_rows, axis=0), 0.0).astype(BF16)
    o = jnp.dot(jnp.concatenate([a0, a1, qhat], axis=1), jnp.concatenate([v0, v1, s_old_b], axis=0),
                preferred_element_type=F32)
    o2 = o * o
    hi = o2.astype(BF16)
    lo = (o2 - hi.astype(F32)).astype(BF16)
    ms = jnp.dot(jnp.concatenate([hi, lo], axis=1), jnp.concatenate([bd_mean, bd_mean], axis=0),
                 preferred_element_type=F32)
    outs = [o * lax.rsqrt(ms + RMS_EPS)]
    ri = lax.broadcasted_iota(jnp.int32, (LANES, LANES), 0)
    ci = lax.broadcasted_iota(jnp.int32, (LANES, LANES), 1)
    e_col = jnp.sum(jnp.where(ri == ci, jnp.broadcast_to(jnp.exp(b_end), (LANES, LANES)), 0.0), axis=1, keepdims=True)
    upd = lax.dot_general(khat, vb, TN_DIMS, preferred_element_type=F32)
    s_new = e_col * s_old + jnp.where((ri // D_HEAD) == (ci // D_HEAD), upd, 0.0)
    return jnp.concatenate(outs, axis=0), s_new


def _hgrn_prompt_kernel(z_ref, lb_ref, gw_ref, tri_ref, bd_ref, o_ref, s_ref, s_sc, *, chunk, sub):
    j = pl.program_id(1)

    @pl.when(j == 0)
    def _():
        s_sc[...] = jnp.zeros(s_sc.shape, F32)

    s_olds = [s_sc[p] for p in range(N_HEADS // 2)]
    z = z_ref[...]
    lb = lb_ref[...]
    fg = lb + (1.0 - lb) * jax.nn.sigmoid(z[:, D_REC:2 * D_REC])
    kk = 1.0 - fg
    q = _silu(z[:, :D_REC])
    b = jnp.dot(tri_ref[...], jnp.log(fg), precision=HIGHEST, preferred_element_type=F32)
    outs = []
    for p in range(N_HEADS // 2):
        ps = slice(p * LANES, (p + 1) * LANES)
        o, s_new = _hgrn_pair(q[:, ps], kk[:, ps], b[:, ps], z[:, 2 * D_REC + p * LANES:2 * D_REC + (p + 1) * LANES],
                              bd_ref[...], s_olds[p], chunk=chunk, sub=sub)
        gate = _silu(z[:, 3 * D_REC + p * LANES:3 * D_REC + (p + 1) * LANES])
        outs.append(((o * gw_ref[...]) * gate).astype(o_ref.dtype))
        s_olds[p] = s_new
    for p in range(N_HEADS // 2):
        s_sc[p] = s_olds[p]
    o_ref[...] = jnp.concatenate(outs, axis=1)

    @pl.when(j == pl.num_programs(1) - 1)
    def _():
        for h in range(N_HEADS):
            d0 = (h % 2) * D_HEAD
            s_ref[0, h] = s_sc[h // 2][d0:d0 + D_HEAD, d0:d0 + D_HEAD]


def _hgrn_prompt(zrec, lb, gw, batch, t, chunk, sub):
    nc = t // chunk
    idx = jnp.arange(chunk)
    tri = (idx[:, None] >= idx[None, :]).astype(F32)
    li = jnp.arange(LANES) // D_HEAD
    bd_mean = ((li[:, None] == li[None, :]).astype(F32) / D_HEAD).astype(BF16)
    gw2 = jnp.concatenate([gw, gw], axis=1)
    return pl.pallas_call(
        functools.partial(_hgrn_prompt_kernel, chunk=chunk, sub=sub),
        grid=(batch, nc),
        in_specs=[pl.BlockSpec((chunk, 4 * D_REC), lambda b, j: (b * nc + j, 0)),
                  pl.BlockSpec((1, D_REC), lambda b, j: (0, 0)),
                  pl.BlockSpec((1, LANES), lambda b, j: (0, 0)),
                  pl.BlockSpec((chunk, chunk), lambda b, j: (0, 0)),
                  pl.BlockSpec((LANES, LANES), lambda b, j: (0, 0))],
        out_specs=[pl.BlockSpec((chunk, D_REC), lambda b, j: (b * nc + j, 0)),
                   pl.BlockSpec((1, N_HEADS, D_HEAD, D_HEAD), lambda b, j: (b, 0, 0, 0))],
        out_shape=[jax.ShapeDtypeStruct((batch * t, D_REC), BF16),
                   jax.ShapeDtypeStruct((batch, N_HEADS, D_HEAD, D_HEAD), F32)],
        scratch_shapes=[pltpu.VMEM((N_HEADS // 2, LANES, LANES), F32)],
        compiler_params=_params(("parallel", "arbitrary"), 32),
        name="hgrn_prompt",
    )(zrec, lb, gw2, tri, bd_mean)


def _hgrn_sample_kernel(z_ref, s0_ref, lb_ref, gw_ref, tri_ref, o_ref, s_ref, s_sc, o_sc, *, seqs, t, chunk):
    lb = lb_ref[...]
    gw = gw_ref[...]
    tri = tri_ref[...]
    pad = jnp.zeros((chunk - t, D_REC), F32)
    for s in range(seqs):
        z = z_ref[s]
        s_sc[...] = s0_ref[s]
        zq, zf, zi = (jnp.concatenate([z[:, u * D_REC:(u + 1) * D_REC], pad], axis=0) for u in range(3))
        _hgrn_chunk(zq, zf, zi, lb, gw, tri, s_sc, o_sc, chunk=chunk, sub=chunk, n_valid=t)
        o_ref[s] = (o_sc[:t, :] * _silu(z[:, 3 * D_REC:])).astype(o_ref.dtype)
        s_ref[s] = s_sc[...]


def _hgrn_sample(zrec, s0, lb, gw, batch, t, seqs):
    chunk = 8
    idx = jnp.arange(chunk)
    tri = (idx[:, None] >= idx[None, :]).astype(F32)
    return pl.pallas_call(
        functools.partial(_hgrn_sample_kernel, seqs=seqs, t=t, chunk=chunk),
        grid=(batch // seqs,),
        in_specs=[pl.BlockSpec((seqs, t, 4 * D_REC), lambda b: (b, 0, 0)),
                  pl.BlockSpec((seqs, N_HEADS, D_HEAD, D_HEAD), lambda b: (b, 0, 0, 0)),
                  pl.BlockSpec((1, D_REC), lambda b: (0, 0)),
                  pl.BlockSpec((1, D_HEAD), lambda b: (0, 0)),
                  pl.BlockSpec((chunk, chunk), lambda b: (0, 0))],
        out_specs=[pl.BlockSpec((seqs, t, D_REC), lambda b: (b, 0, 0)),
                   pl.BlockSpec((seqs, N_HEADS, D_HEAD, D_HEAD), lambda b: (b, 0, 0, 0))],
        out_shape=[jax.ShapeDtypeStruct((batch, t, D_REC), BF16),
                   jax.ShapeDtypeStruct((batch, N_HEADS, D_HEAD, D_HEAD), F32)],
        scratch_shapes=[pltpu.VMEM((N_HEADS, D_HEAD, D_HEAD), F32), pltpu.VMEM((chunk, D_REC), F32)],
        compiler_params=_params(("parallel",), 32),
        name="hgrn_sample",
    )(zrec.reshape(batch, t, 4 * D_REC), s0, lb, gw, tri)


def _fox_sample_kernel(pt_ref, q_ref, kn_ref, vn_ref, lft_ref, gs_ref, ck_ref, cv_ref, clf_ref, o_ref,
                       kbuf, vbuf, lbuf, sem, *, t, ppb, n_pages):
    b = pl.program_id(0)
    nb = pl.num_programs(0)
    nchunk = n_pages // ppb

    def chunk_copies(bb, c, slot):
        for u in range(ppb):
            page = pt_ref[bb, n_pages - 1 - (c * ppb + u)]
            yield pltpu.make_async_copy(ck_ref.at[page], kbuf.at[slot, u], sem.at[0, slot])
            yield pltpu.make_async_copy(cv_ref.at[page], vbuf.at[slot, u], sem.at[1, slot])
            yield pltpu.make_async_copy(clf_ref.at[page], lbuf.at[slot, u], sem.at[2, slot])

    def start_chunk(bb, c, slot):
        for cp in chunk_copies(bb, c, slot):
            cp.start()

    def wait_chunk(slot):
        pltpu.make_async_copy(ck_ref.at[pl.ds(0, ppb)], kbuf.at[slot], sem.at[0, slot]).wait()
        pltpu.make_async_copy(cv_ref.at[pl.ds(0, ppb)], vbuf.at[slot], sem.at[1, slot]).wait()
        pltpu.make_async_copy(clf_ref.at[pl.ds(0, ppb)], lbuf.at[slot], sem.at[2, slot]).wait()

    @pl.when(b == 0)
    def _():
        start_chunk(0, 0, 0)

    nrow = t * N_HEADS
    lane = lax.broadcasted_iota(jnp.int32, (nrow, D_ATT), 1)
    rowi = lax.broadcasted_iota(jnp.int32, (nrow, D_ATT), 0)
    head_lanes = (lane // D_HEAD) == (rowi % N_HEADS)
    q = q_ref[0]
    qbd = jnp.concatenate([jnp.broadcast_to(q[u:u + 1, :], (N_HEADS, D_ATT)) for u in range(t)], axis=0)
    qbd = jnp.where(head_lanes, qbd, jnp.zeros_like(qbd))

    lft = lft_ref[0]
    cols = [lft[:, 0:1]]
    for u in range(1, t):
        cols.append(cols[-1] + lft[:, u:u + 1])
    cq_t = jnp.concatenate(cols, axis=1)
    cq_col = jnp.concatenate(cols, axis=0)
    zpad = jnp.zeros((N_HEADS - t, D_ATT), F32)
    kn = jnp.concatenate([kn_ref[0], zpad], axis=0).astype(BF16)
    vn = jnp.concatenate([vn_ref[0], zpad], axis=0).astype(BF16)
    cq_t8 = jnp.concatenate([cq_t, jnp.zeros((N_HEADS, N_HEADS - t), F32)], axis=1)
    s = lax.dot_general(qbd, kn, NT_DIMS, preferred_element_type=F32)
    s = s + cq_col - jnp.concatenate([cq_t8] * t, axis=0)
    r2 = lax.broadcasted_iota(jnp.int32, (nrow, N_HEADS), 0) // N_HEADS
    c2 = lax.broadcasted_iota(jnp.int32, (nrow, N_HEADS), 1)
    s = jnp.where(c2 <= r2, s, NEG_BIG)
    m = jnp.max(s, axis=-1, keepdims=True)
    p0 = jnp.exp(s - m)
    l = jnp.sum(p0, axis=-1, keepdims=True)
    acc = jnp.dot(p0.astype(BF16), vn, preferred_element_type=F32)
    carry = jnp.zeros((N_HEADS, 1), F32)

    for c in range(nchunk):
        slot = (b * nchunk + c) % 2
        if c + 1 < nchunk:
            start_chunk(b, c + 1, 1 - slot)
        else:
            @pl.when(b + 1 < nb)
            def _():
                start_chunk(b + 1, 0, 1 - slot)
        wait_chunk(slot)
        lf_pages = [lbuf[slot, u] for u in range(ppb)]
        insuf = jnp.dot(jnp.concatenate(lf_pages, axis=0), gs_ref[...], precision=HIGHEST,
                        preferred_element_type=F32)
        biases = []
        for u in range(ppb):
            biases.append(insuf[u * N_HEADS:(u + 1) * N_HEADS, :] + carry)
            carry = carry + jnp.sum(lf_pages[u], axis=-1, keepdims=True)
        bias = jnp.concatenate(biases, axis=1)
        bias = jnp.concatenate([bias] * t, axis=0) + cq_col
        kcat = jnp.concatenate([kbuf[slot, u].astype(BF16) for u in range(ppb)], axis=1)
        vcat = jnp.concatenate([vbuf[slot, u].astype(BF16) for u in range(ppb)], axis=1)
        s = jnp.dot(qbd, kcat, preferred_element_type=F32) + bias
        m_new = jnp.maximum(m, jnp.max(s, axis=-1, keepdims=True))
        alpha = jnp.exp(m - m_new)
        pr = jnp.exp(s - m_new)
        l = alpha * l + jnp.sum(pr, axis=-1, keepdims=True)
        acc = alpha * acc + lax.dot_general(pr.astype(BF16), vcat, NT_DIMS, preferred_element_type=F32)
        m = m_new

    o = jnp.where(head_lanes, acc / l, 0.0)
    o_ref[0] = jnp.sum(o.reshape(t, N_HEADS, D_ATT), axis=1).astype(o_ref.dtype)


def _fox_sample(page_table, q, kn, vn, lft, cache_k, cache_v, cache_lft, batch, t, ppb):
    n_pages = page_table.shape[1]
    ppb = min(ppb, n_pages)
    idx = jnp.arange(PAGE_SIZE)
    gs = (idx[:, None] > idx[None, :]).astype(F32)
    seq = lambda w: pl.BlockSpec((1, t, w), lambda b, pt: (b, 0, 0))
    hbm = pl.BlockSpec(memory_space=pl.ANY)
    grid_spec = pltpu.PrefetchScalarGridSpec(
        num_scalar_prefetch=1,
        grid=(batch,),
        in_specs=[seq(D_ATT), seq(D_ATT), seq(D_ATT),
                  pl.BlockSpec((1, N_HEADS, t), lambda b, pt: (b, 0, 0)),
                  pl.BlockSpec((PAGE_SIZE, PAGE_SIZE), lambda b, pt: (0, 0)),
                  hbm, hbm, hbm],
        out_specs=pl.BlockSpec((1, t, D_ATT), lambda b, pt: (b, 0, 0)),
        scratch_shapes=[pltpu.VMEM((2, ppb, D_ATT, PAGE_SIZE), F32), pltpu.VMEM((2, ppb, D_ATT, PAGE_SIZE), F32),
                        pltpu.VMEM((2, ppb, N_HEADS, PAGE_SIZE), F32), pltpu.SemaphoreType.DMA((3, 2))],
    )
    return pl.pallas_call(
        functools.partial(_fox_sample_kernel, t=t, ppb=ppb, n_pages=n_pages),
        grid_spec=grid_spec,
        out_shape=jax.ShapeDtypeStruct((batch, t, D_ATT), BF16),
        compiler_params=_params(("arbitrary",), 48),
        name="fox_sample",
    )(page_table, q.reshape(batch, t, D_ATT), kn.reshape(batch, t, D_ATT), vn.reshape(batch, t, D_ATT), lft, gs,
      cache_k, cache_v, cache_lft)


def _outproj_router_kernel(x_ref, orec_ref, oatt_ref, wo_ref, nw_ref, wr_ref, br_ref, cnt0_ref, tril_ref,
                           h_ref, xn_ref, idx_ref, gate_ref, rank_ref, cnt_ref, base_sc):
    i = pl.program_id(0)

    @pl.when(i == 0)
    def _():
        base_sc[...] = cnt0_ref[...]

    tm = x_ref.shape[0]
    h = (x_ref[...]
         + jnp.dot(orec_ref[...], wo_ref[:D_REC, :], preferred_element_type=F32)
         + jnp.dot(oatt_ref[...], wo_ref[D_REC:, :], preferred_element_type=F32))
    h_ref[...] = h
    r = lax.rsqrt(jnp.mean(h * h, axis=-1, keepdims=True) + RMS_EPS)
    xn = (h * r) * nw_ref[...]
    xn_ref[...] = xn
    x_hi = xn.astype(BF16)
    x_lo = (xn - x_hi.astype(F32)).astype(BF16)
    logits = (jnp.dot(x_hi, wr_ref[0], preferred_element_type=F32)
              + (jnp.dot(x_lo, wr_ref[0], preferred_element_type=F32)
                 + jnp.dot(x_hi, wr_ref[1], preferred_element_type=F32))) + br_ref[...]
    lane = lax.broadcasted_iota(jnp.int32, (tm, LANES), 1)
    vals, hots = [], []
    idx_out = jnp.zeros((tm, LANES), jnp.int32)
    for k in range(TOP_K):
        m = jnp.max(logits, axis=-1, keepdims=True)
        sel = jnp.min(jnp.where(logits == m, lane, LANES), axis=-1, keepdims=True)
        hot = lane == sel
        vals.append(m)
        hots.append(hot)
        idx_out = jnp.where(lane == k, sel, idx_out)
        logits = jnp.where(hot, -jnp.inf, logits)
    idx_ref[...] = idx_out
    es = [jnp.exp(v - vals[0]) for v in vals]
    denom = es[0] + es[1] + es[2] + es[3]
    gate_out = jnp.zeros((tm, LANES), F32)
    for k in range(TOP_K):
        gate_out = jnp.where(lane == k, es[k] / denom, gate_out)
    gate_ref[...] = gate_out
    base = base_sc[...]
    rank_out = jnp.zeros((tm, LANES), jnp.int32)
    for k in range(TOP_K):
        hot_f = hots[k].astype(F32)
        before = jnp.dot(tril_ref[...], hot_f.astype(BF16), preferred_element_type=F32) + base
        rank = jnp.sum(jnp.where(hots[k], before, 0.0), axis=-1, keepdims=True)
        rank_out = jnp.where(lane == k, rank.astype(jnp.int32), rank_out)
        base = base + jnp.sum(hot_f, axis=0, keepdims=True)
    rank_ref[...] = rank_out
    base_sc[...] = base
    cnt_ref[...] = base


def _outproj_router(x, orec, oatt, wo, nw, wr, br, cnt0, tm):
    n = x.shape[0]
    idx = jnp.arange(tm)
    tril = (idx[:, None] > idx[None, :]).astype(BF16)
    full = lambda a: pl.BlockSpec(a.shape, lambda i: (0,) * a.ndim)
    row = lambda w: pl.BlockSpec((tm, w), lambda i: (i, 0))
    return pl.pallas_call(
        _outproj_router_kernel,
        grid=(n // tm,),
        in_specs=[row(D_MODEL), row(D_REC), row(D_ATT), full(wo), full(nw), full(wr), full(br), full(cnt0), full(tril)],
        out_specs=[row(D_MODEL), row(D_MODEL), row(LANES), row(LANES), row(LANES),
                   pl.BlockSpec((1, LANES), lambda i: (0, 0))],
        out_shape=[jax.ShapeDtypeStruct((n, D_MODEL), F32), jax.ShapeDtypeStruct((n, D_MODEL), F32),
                   jax.ShapeDtypeStruct((n, LANES), jnp.int32), jax.ShapeDtypeStruct((n, LANES), F32),
                   jax.ShapeDtypeStruct((n, LANES), jnp.int32), jax.ShapeDtypeStruct((1, LANES), F32)],
        scratch_shapes=[pltpu.VMEM((1, LANES), F32)],
        compiler_params=_params(("arbitrary",), 40),
        name="outproj_router",
    )(x, orec, oatt, wo, nw, wr, br, cnt0, tril)


def _dispatch_kernel(gend_ref, gsize_ref, total_ref, dest_ref, xp_ref, xs_ref, out_ref, zbuf, sem, zsem,
                     *, n_prompt_tiles, tile_rows):
    i = pl.program_id(0)
    n_tiles = out_ref.shape[0] // tile_rows

    @pl.when(i == 0)
    def _():
        zbuf[...] = jnp.zeros(zbuf.shape, zbuf.dtype)

        def zero_tile(start):
            cp = pltpu.make_async_copy(zbuf, out_ref.at[pl.ds(pl.multiple_of(start, tile_rows), tile_rows), :], zsem)
            cp.start()
            cp.wait()

        for e in range(N_EXPERTS):
            @pl.when(gsize_ref[e] > 0)
            def _():
                zero_tile(gend_ref[e] - tile_rows)

        def tail(tt, carry):
            zero_tile(tt * tile_rows)
            return carry

        lax.fori_loop(total_ref[0] // tile_rows, n_tiles, tail, 0)

    def scatter(src_ref):
        n = src_ref.shape[0] * SUBLANES

        def issue(g, carry):
            for u in range(SUBLANES):
                for k in range(TOP_K):
                    pltpu.make_async_copy(src_ref.at[g, pl.ds(u, 1), :],
                                          out_ref.at[pl.ds(dest_ref[0, 0, (g * SUBLANES + u) * TOP_K + k], 1), :],
                                          sem).start(priority=k % 2)
            return carry

        lax.fori_loop(0, n // SUBLANES, issue, 0)
        for k in range(TOP_K):
            pltpu.make_async_copy(zbuf.at[pl.ds(0, n), :], out_ref.at[pl.ds(0, n), :], sem).wait()

    @pl.when(i < n_prompt_tiles)
    def _():
        scatter(xp_ref)

    @pl.when(i == n_prompt_tiles)
    def _():
        scatter(xs_ref)


def _dispatch(group_end, group_size, dest_p, dest_s, xn_p, xn_s, n_rows, tm, tile_rows):
    n_p, n_s = xn_p.shape[0], xn_s.shape[0]
    npt = n_p // tm
    dest = jnp.concatenate([dest_p.reshape(npt, tm * TOP_K),
                            jnp.pad(dest_s, (0, (tm - n_s) * TOP_K)).reshape(1, tm * TOP_K)], axis=0)
    grid_spec = pltpu.PrefetchScalarGridSpec(
        num_scalar_prefetch=3,
        grid=(npt + 1,),
        in_specs=[pl.BlockSpec((1, 1, tm * TOP_K), lambda i, *_: (i, 0, 0), memory_space=pltpu.SMEM),
                  pl.BlockSpec((tm // SUBLANES, SUBLANES, D_MODEL), lambda i, *_: (jnp.minimum(i, npt - 1), 0, 0)),
                  pl.BlockSpec((n_s // SUBLANES, SUBLANES, D_MODEL), lambda i, *_: (0, 0, 0))],
        out_specs=pl.BlockSpec(memory_space=pl.ANY),
        scratch_shapes=[pltpu.VMEM((tile_rows, D_MODEL), F32), pltpu.SemaphoreType.DMA(()),
                        pltpu.SemaphoreType.DMA(())],
    )
    return pl.pallas_call(
        functools.partial(_dispatch_kernel, n_prompt_tiles=npt, tile_rows=tile_rows),
        grid_spec=grid_spec,
        out_shape=jax.ShapeDtypeStruct((n_rows, D_MODEL), F32),
        compiler_params=_params(("arbitrary",), 32),
        name="moe_dispatch",
    )(group_end, group_size, group_end[-1:], dest.reshape(npt + 1, 1, tm * TOP_K),
      xn_p.reshape(n_p // SUBLANES, SUBLANES, D_MODEL), xn_s.reshape(n_s // SUBLANES, SUBLANES, D_MODEL))


def _expert_mlp_kernel(te_ref, tv_ref, tf_ref, x_ref, wg_ref, bg_ref, wu_ref, bu_ref, wd_ref, bd_ref, y_ref,
                       wg_sc, wu_sc, wd_sc):
    i = pl.program_id(0)
    del te_ref

    @pl.when(tf_ref[i] != 0)
    def _():
        wg_sc[...] = wg_ref[0].astype(BF16)
        wu_sc[...] = wu_ref[0].astype(BF16)
        wd_sc[...] = wd_ref[0].astype(BF16)

    @pl.when(tv_ref[i] != 0)
    def _():
        x = x_ref[...].astype(BF16)
        gate = jnp.minimum(jnp.dot(x, wg_sc[...], preferred_element_type=F32) + bg_ref[0], SWIGLU_LIMIT)
        up = jnp.clip(jnp.dot(x, wu_sc[...], preferred_element_type=F32) + bu_ref[0], -SWIGLU_LIMIT, SWIGLU_LIMIT)
        glu = gate * jax.nn.sigmoid(gate * SWIGLU_ALPHA)
        mid = ((up + 1.0) * glu).astype(BF16)
        y_ref[...] = jnp.dot(mid, wd_sc[...], preferred_element_type=F32) + bd_ref[0]

    @pl.when(tv_ref[i] == 0)
    def _():
        y_ref[...] = jnp.zeros(y_ref.shape, y_ref.dtype)


def _expert_mlp(tile_expert, tile_valid, tile_first, xs, wg, bg, wu, bu, wd, bd, tm):
    p = xs.shape[0]
    wspec = lambda: pl.BlockSpec((1, D_MODEL, D_FF), lambda i, te, tv, tf: (te[i], 0, 0))
    bspec = lambda: pl.BlockSpec((1, 1, D_FF), lambda i, te, tv, tf: (te[i], 0, 0))
    grid_spec = pltpu.PrefetchScalarGridSpec(
        num_scalar_prefetch=3,
        grid=(p // tm,),
        in_specs=[pl.BlockSpec((tm, D_MODEL), lambda i, te, tv, tf: (i * tv[i], 0)),
                  wspec(), bspec(), wspec(), bspec(), wspec(), bspec()],
        out_specs=pl.BlockSpec((tm, D_MODEL), lambda i, te, tv, tf: (i, 0)),
        scratch_shapes=[pltpu.VMEM((D_MODEL, D_FF), BF16), pltpu.VMEM((D_MODEL, D_FF), BF16),
                        pltpu.VMEM((D_FF, D_MODEL), BF16)],
    )
    return pl.pallas_call(
        _expert_mlp_kernel,
        grid_spec=grid_spec,
        out_shape=jax.ShapeDtypeStruct((p, D_MODEL), F32),
        compiler_params=_params(("arbitrary",), 56),
        name="expert_mlp",
    )(tile_expert, tile_valid, tile_first, xs, wg, bg, wu, bu, wd, bd)


def _combine_kernel(dest_ref, dest_next_ref, gate_ref, h_ref, nw_ref, ys_ref, y_ref, buf, sem):
    i = pl.program_id(0)
    tm = h_ref.shape[0]
    slot = i % 2

    def request(dref, s):
        def issue(g, carry):
            for u in range(SUBLANES):
                for k in range(TOP_K):
                    pltpu.make_async_copy(ys_ref.at[pl.ds(dref[0, 0, (g * SUBLANES + u) * TOP_K + k], 1), :],
                                          buf.at[s, k, g, pl.ds(u, 1), :], sem.at[s]).start(priority=k % 2)
            return carry

        lax.fori_loop(0, tm // SUBLANES, issue, 0)

    @pl.when(i == 0)
    def _():
        request(dest_ref, 0)

    for s in range(2):
        @pl.when(slot == s)
        def _():
            @pl.when(i + 1 < pl.num_programs(0))
            def _():
                request(dest_next_ref, 1 - s)

            for k in range(TOP_K):
                pltpu.make_async_copy(ys_ref.at[pl.ds(0, tm), :], y_ref, sem.at[s]).wait()

    gates = gate_ref[...]
    moe = gates[:, 0:1] * buf[slot, 0].reshape(tm, D_MODEL)
    for k in range(1, TOP_K):
        moe = moe + gates[:, k:k + 1] * buf[slot, k].reshape(tm, D_MODEL)
    h = h_ref[...] + moe
    r = lax.rsqrt(jnp.mean(h * h, axis=-1, keepdims=True) + RMS_EPS)
    y_ref[...] = (h * r) * nw_ref[...]


def _combine(dest, gates, h, nw, ys, tm):
    n = h.shape[0]
    nt = n // tm
    dest3 = dest.reshape(nt, 1, tm * TOP_K)
    return pl.pallas_call(
        _combine_kernel,
        grid=(nt,),
        in_specs=[pl.BlockSpec((1, 1, tm * TOP_K), lambda i: (i, 0, 0), memory_space=pltpu.SMEM),
                  pl.BlockSpec((1, 1, tm * TOP_K), lambda i: (jnp.minimum(i + 1, nt - 1), 0, 0),
                               memory_space=pltpu.SMEM),
                  pl.BlockSpec((tm, LANES), lambda i: (i, 0)),
                  pl.BlockSpec((tm, D_MODEL), lambda i: (i, 0)),
                  pl.BlockSpec((1, D_MODEL), lambda i: (0, 0)),
                  pl.BlockSpec(memory_space=pl.ANY)],
        out_specs=pl.BlockSpec((tm, D_MODEL), lambda i: (i, 0)),
        out_shape=jax.ShapeDtypeStruct((n, D_MODEL), F32),
        scratch_shapes=[pltpu.VMEM((2, TOP_K, tm // SUBLANES, SUBLANES, D_MODEL), F32),
                        pltpu.SemaphoreType.DMA((2,))],
        compiler_params=_params(("arbitrary",), 32),
        name="moe_combine",
    )(dest3, dest3, gates, h, nw, ys)


TM_TOKENS = 512
TM_EXPERT = 512
TM_COMBINE = 256
TQ_PROMPT = 512
TK_PROMPT = 512
HGRN_CHUNK = 128
HGRN_SUB = 32
PAGES_PER_STEP = 16
HGRN_SAMPLE_SEQS = 8


def _step(x_prompt, x_sample, cache_k, cache_v, cache_logf, state_hgrn, page_table,
          norm_mix_w, w_in, b_fa, lb_param, gnorm_w, w_out, norm_ffn_w,
          w_router, b_router, w_gate, b_gate, w_up, b_up, w_down, b_down, norm_final_w):
    batch, t, _ = x_prompt.shape
    dec_batch, dec_t, _ = x_sample.shape
    n_p, n_s = batch * t, dec_batch * dec_t
    n_pool = cache_k.shape[1]

    w_in_b = w_in[0].astype(BF16)
    wrec = w_in_b[:, :4 * D_REC]
    watt = w_in_b[:, 4 * D_REC:4 * D_REC + 3 * D_ATT]
    wfa = jnp.pad(w_in_b[:, 4 * D_REC + 3 * D_ATT:], ((0, 0), (0, LANES - N_HEADS)))
    wfat = jnp.pad(w_in_b[:, 4 * D_REC + 3 * D_ATT:].T, ((0, 16 - N_HEADS), (0, 0)))
    bfa = b_fa[0].reshape(1, N_HEADS)
    bfat = b_fa[0].reshape(N_HEADS, 1)
    nw_mix = norm_mix_w[0].reshape(1, D_MODEL)
    nw_ffn = norm_ffn_w[0].reshape(1, D_MODEL)
    nw_fin = norm_final_w.reshape(1, D_MODEL)
    lb = jnp.cumsum(jax.nn.softmax(lb_param.astype(F32), axis=0), axis=0)[0].reshape(1, D_REC)
    gw = gnorm_w.reshape(1, D_HEAD)
    wo = w_out[0].astype(BF16)
    wr_f = jnp.pad(w_router[0], ((0, 0), (0, LANES - N_EXPERTS)))
    wr_hi = wr_f.astype(BF16)
    wr = jnp.stack([wr_hi, (wr_f - wr_hi.astype(F32)).astype(BF16)])
    br = jnp.pad(b_router[0], (0, LANES - N_EXPERTS), constant_values=NEG_BIG).reshape(1, LANES)
    wg, wu, wd = w_gate[0], w_up[0], w_down[0]
    bg, bu, bd = (b[0].reshape(N_EXPERTS, 1, -1) for b in (b_gate, b_up, b_down))

    xp = x_prompt.reshape(n_p, D_MODEL)
    xs = x_sample.reshape(n_s, D_MODEL)
    tm_p, tm_s = min(TM_TOKENS, n_p), min(TM_TOKENS, n_s)
    proj = functools.partial(_inproj, nw=nw_mix, wrec=wrec, watt=watt, wfa=wfa, wfat=wfat, bfa=bfa, bfat=bfat)
    zrec_p, q_p, kt_p, vt_p, kb_p, vb_p, lf_p, lft_p = proj(xp, tm=tm_p, seq_len=t)
    zrec_s, q_s, k_s, v_s, _, _, lf_s, lft_s = proj(xs, tm=tm_s)

    c_p, ct_p = _cumsum_logf(lf_p, lft_p, batch, t)
    oatt_p = _fox_prompt(q_p, kb_p, vb_p, c_p, ct_p, batch, t, TQ_PROMPT, TK_PROMPT)
    orec_p, state_p = _hgrn_prompt(zrec_p, lb, gw, batch, t, HGRN_CHUNK, HGRN_SUB)

    orec_s, state_s = _hgrn_sample(zrec_s, state_hgrn[0], lb, gw, dec_batch, dec_t, HGRN_SAMPLE_SEQS)
    lft_s3 = lft_s.reshape(N_HEADS, dec_batch, dec_t).transpose(1, 0, 2)
    cache_lft = cache_logf[0].transpose(0, 2, 1)
    oatt_s = _fox_sample(page_table, q_s, k_s, v_s, lft_s3,
                         cache_k[0].transpose(0, 2, 3, 1).reshape(n_pool, D_ATT, PAGE_SIZE),
                         cache_v[0].transpose(0, 2, 3, 1).reshape(n_pool, D_ATT, PAGE_SIZE),
                         cache_lft, dec_batch, dec_t, PAGES_PER_STEP)

    route = functools.partial(_outproj_router, wo=wo, nw=nw_ffn, wr=wr, br=br)
    h_p, xn_p, idx_p, gate_p, rank_p, cnt_p = route(xp, orec_p, oatt_p, cnt0=jnp.zeros((1, LANES), F32), tm=tm_p)
    h_s, xn_s, idx_s, gate_s, rank_s, cnt = route(xs, orec_s.reshape(n_s, D_REC), oatt_s.reshape(n_s, D_ATT),
                                                  cnt0=cnt_p, tm=tm_s)

    counts = cnt[0, :N_EXPERTS].astype(jnp.int32)
    padded = ((counts + TM_EXPERT - 1) // TM_EXPERT) * TM_EXPERT
    group_end = jnp.cumsum(padded)
    group_start = group_end - padded
    n_rows = (n_p + n_s) * TOP_K + N_EXPERTS * TM_EXPERT
    tile_start = jnp.arange(n_rows // TM_EXPERT, dtype=jnp.int32) * TM_EXPERT
    tile_expert = jnp.minimum(jnp.sum((tile_start[:, None] >= group_end[None, :]).astype(jnp.int32), axis=1),
                              N_EXPERTS - 1)
    tile_valid = (tile_start < group_end[-1]).astype(jnp.int32)
    tile_first = jnp.concatenate([jnp.ones((1,), jnp.int32),
                                  (tile_expert[1:] != tile_expert[:-1]).astype(jnp.int32)])
    dest_p = (group_start[idx_p[:, :TOP_K]] + rank_p[:, :TOP_K]).reshape(-1)
    dest_s = (group_start[idx_s[:, :TOP_K]] + rank_s[:, :TOP_K]).reshape(-1)

    x_sorted = _dispatch(group_end.astype(jnp.int32), padded, dest_p, dest_s, xn_p, xn_s, n_rows, tm_p, TM_EXPERT)
    y_sorted = _expert_mlp(tile_expert, tile_valid, tile_first, x_sorted, wg, bg, wu, bu, wd, bd, TM_EXPERT)
    y_p = _combine(dest_p, gate_p, h_p, nw_fin, y_sorted, min(TM_COMBINE, n_p))
    y_s = _combine(dest_s, gate_s, h_s, nw_fin, y_sorted, n_s)

    heads = lambda a, b_, t_: a.reshape(1, b_, t_, N_HEADS, D_HEAD)
    heads_t = lambda a: a.reshape(1, batch, N_HEADS, D_HEAD, t).transpose(0, 1, 4, 2, 3)
    return (y_p.reshape(batch, t, D_MODEL), y_s.reshape(dec_batch, dec_t, D_MODEL),
            heads_t(kt_p), heads_t(vt_p), lf_p.reshape(1, batch, t, N_HEADS), state_p[None],
            heads(k_s, dec_batch, dec_t), heads(v_s, dec_batch, dec_t), lf_s.reshape(1, dec_batch, dec_t, N_HEADS),
            state_s[None])


def kernel(x_prompt, x_sample, cache_k, cache_v, cache_logf, state_hgrn, page_table, norm_mix_w, w_in, b_fa, lb_param, gnorm_w, w_out, norm_ffn_w, w_router, b_router, w_gate, b_gate, w_up, b_up, w_down, b_down, norm_final_w):
    return _step(x_prompt, x_sample, cache_k, cache_v, cache_logf, state_hgrn, page_table,
                 norm_mix_w, w_in, b_fa, lb_param, gnorm_w, w_out, norm_ffn_w,
                 w_router, b_router, w_gate, b_gate, w_up, b_up, w_down, b_down, norm_final_w)
```

```python
import functools

import jax
import jax.numpy as jnp
from jax import lax
from jax.experimental import pallas as pl
from jax.experimental.pallas import tpu as pltpu

D_MODEL = 1024
D_REC = 512
D_ATT = 512
N_HEADS = 8
D_HEAD = 64
N_EXPERTS = 32
TOP_K = 4
D_FF = 1024
PAGE_SIZE = 128
SWIGLU_LIMIT = 7.0
SWIGLU_ALPHA = 1.702
RMS_EPS = 1e-5

LANES = 128
SUBLANES = 8
NEG_BIG = -1e30
MIB = 1024 * 1024

F32 = jnp.float32
BF16 = jnp.bfloat16
HIGHEST = lax.Precision.HIGHEST

NT_DIMS = (((1,), (1,)), ((), ()))
TN_DIMS = (((0,), (0,)), ((), ()))


def _log_sigmoid(x):
    return jnp.minimum(x, 0.0) - jnp.log1p(jnp.exp(-jnp.abs(x)))


def _silu(x):
    return x * jax.nn.sigmoid(x)


def _split3(x):
    parts = []
    for _ in range(3):
        part = x.astype(BF16)
        parts.append(part)
        x = x - part.astype(F32)
    return parts


def _params(sem, vmem_mib):
    return pltpu.CompilerParams(dimension_semantics=sem, vmem_limit_bytes=vmem_mib * MIB)


def _inproj_kernel(x_ref, nw_ref, wrec_ref, watt_ref, wfa_ref, wfat_ref, bfa_ref, bfat_ref,
                   zrec_ref, q_ref, k_ref, v_ref, kb_ref, vb_ref, lf_ref, lft_ref, *, kv_transposed):
    x = x_ref[...]
    r = lax.rsqrt(jnp.mean(x * x, axis=-1, keepdims=True) + RMS_EPS)
    xn = ((x * r) * nw_ref[...]).astype(BF16)
    zrec_ref[...] = jnp.dot(xn, wrec_ref[...], preferred_element_type=F32)
    za = jnp.dot(xn, watt_ref[...], preferred_element_type=F32)
    q_ref[...] = (za[:, :D_ATT] * (D_HEAD ** -0.5)).astype(BF16)
    k = za[:, D_ATT:2 * D_ATT]
    v = za[:, 2 * D_ATT:]
    if kv_transposed:
        k_ref[0] = k.T
        v_ref[0] = v.T
    else:
        k_ref[...] = k
        v_ref[...] = v
    kb_ref[...] = k.astype(BF16)
    vb_ref[...] = v.astype(BF16)
    fa = jnp.dot(xn, wfa_ref[...], preferred_element_type=F32)
    lf_ref[...] = _log_sigmoid(fa[:, :N_HEADS] + bfa_ref[...])
    fat = lax.dot_general(wfat_ref[...], xn, NT_DIMS, preferred_element_type=F32)
    lft_ref[...] = _log_sigmoid(fat[:N_HEADS, :] + bfat_ref[...])


def _inproj(x, nw, wrec, watt, wfa, wfat, bfa, bfat, tm, seq_len=None):
    n = x.shape[0]
    full = lambda a: pl.BlockSpec(a.shape, lambda i: (0,) * a.ndim)
    row = lambda w: pl.BlockSpec((tm, w), lambda i: (i, 0))
    if seq_len is None:
        kv_spec, kv_shape = row(D_ATT), jax.ShapeDtypeStruct((n, D_ATT), F32)
    else:
        nt = seq_len // tm
        kv_spec = pl.BlockSpec((1, D_ATT, tm), lambda i: (i // nt, 0, i % nt))
        kv_shape = jax.ShapeDtypeStruct((n // seq_len, D_ATT, seq_len), F32)
    return pl.pallas_call(
        functools.partial(_inproj_kernel, kv_transposed=seq_len is not None),
        grid=(n // tm,),
        in_specs=[row(D_MODEL), full(nw), full(wrec), full(watt), full(wfa), full(wfat), full(bfa), full(bfat)],
        out_specs=[row(4 * D_REC), row(D_ATT), kv_spec, kv_spec, row(D_ATT), row(D_ATT), row(N_HEADS),
                   pl.BlockSpec((N_HEADS, tm), lambda i: (0, i))],
        out_shape=[jax.ShapeDtypeStruct((n, 4 * D_REC), F32), jax.ShapeDtypeStruct((n, D_ATT), BF16),
                   kv_shape, kv_shape,
                   jax.ShapeDtypeStruct((n, D_ATT), BF16), jax.ShapeDtypeStruct((n, D_ATT), BF16),
                   jax.ShapeDtypeStruct((n, N_HEADS), F32), jax.ShapeDtypeStruct((N_HEADS, n), F32)],
        compiler_params=_params(("parallel",), 52),
        name="inproj",
    )(x, nw, wrec, watt, wfa, wfat, bfa, bfat)


def _cumsum_kernel(lf_ref, lft_ref, tri_ref, triu_ref, c_ref, ct_ref, *, tb):
    t = lf_ref.shape[1]
    tri = tri_ref[...]
    triu = triu_ref[...]
    carry = jnp.zeros((1, N_HEADS), F32)
    carry_t = jnp.zeros((N_HEADS, 1), F32)
    for j in range(t // tb):
        sl = slice(j * tb, (j + 1) * tb)
        cb = sum(jnp.dot(tri, part, preferred_element_type=F32) for part in _split3(lf_ref[0, sl, :])) + carry
        c_ref[0, sl, :] = cb
        carry = cb[tb - 1:tb, :]
        cbt = sum(jnp.dot(part, triu, preferred_element_type=F32) for part in _split3(lft_ref[:, sl])) + carry_t
        ct_ref[:, sl] = cbt
        carry_t = cbt[:, tb - 1:tb]


def _cumsum_logf(lf, lft, batch, t):
    tb = min(t, 512)
    idx = jnp.arange(tb)
    tri = (idx[:, None] >= idx[None, :]).astype(BF16)
    return pl.pallas_call(
        functools.partial(_cumsum_kernel, tb=tb),
        grid=(batch,),
        in_specs=[pl.BlockSpec((1, t, N_HEADS), lambda b: (b, 0, 0)),
                  pl.BlockSpec((N_HEADS, t), lambda b: (0, b)),
                  pl.BlockSpec((tb, tb), lambda b: (0, 0)),
                  pl.BlockSpec((tb, tb), lambda b: (0, 0))],
        out_specs=[pl.BlockSpec((1, t, N_HEADS), lambda b: (b, 0, 0)),
                   pl.BlockSpec((N_HEADS, t), lambda b: (0, b))],
        out_shape=[jax.ShapeDtypeStruct((batch, t, N_HEADS), F32),
                   jax.ShapeDtypeStruct((N_HEADS, batch * t), F32)],
        compiler_params=_params(("parallel",), 32),
        name="cumsum_logf",
    )(lf.reshape(batch, t, N_HEADS), lft, tri, tri.T)


def _fox_prompt_kernel(q_ref, k_ref, v_ref, c_ref, ct_ref, o_ref, m_sc, acc_sc, *, tq, tk):
    p = pl.program_id(1)
    i = pl.program_id(2)
    ratio = tk // tq
    nrep = tk // LANES
    q = q_ref[...]
    lane = lax.broadcasted_iota(jnp.int32, (tq, LANES), 1)
    lane8 = lax.broadcasted_iota(jnp.int32, (tq, N_HEADS), 1)
    lane_k = lax.broadcasted_iota(jnp.int32, (tk, LANES), 1)
    cblk = c_ref[0]
    qs, cqs = [], []
    for hh in range(2):
        qs.append(jnp.where((lane // D_HEAD) == hh, q, jnp.zeros_like(q)))
        cq = jnp.sum(jnp.where(lane8 == 2 * p + hh, cblk, 0.0), axis=-1, keepdims=True)
        cqs.append(jnp.concatenate([jnp.broadcast_to(cq, (tq, LANES))] * nrep, axis=1))
    m_sc[...] = jnp.full(m_sc.shape, NEG_BIG, F32)
    acc_sc[...] = jnp.zeros(acc_sc.shape, F32)

    def step(j, masked):
        off = pl.multiple_of(j * tk, tk)
        kb = k_ref[pl.ds(off, tk), :]
        vb = v_ref[pl.ds(off, tk), :]
        for hh in range(2):
            va = jnp.where((lane_k // D_HEAD) == hh, vb, jnp.ones_like(vb))
            s = lax.dot_general(qs[hh], kb, NT_DIMS, preferred_element_type=F32)
            s = (s + cqs[hh]) - ct_ref[0, hh:hh + 1, pl.ds(off, tk)]
            if masked:
                row = lax.broadcasted_iota(jnp.int32, (tq, tk), 0) + (i % ratio) * tq
                col = lax.broadcasted_iota(jnp.int32, (tq, tk), 1)
                s = jnp.where(col <= row, s, NEG_BIG)
            m_prev = m_sc[hh]
            m_new = jnp.maximum(m_prev, jnp.max(s, axis=-1, keepdims=True))
            alpha = jnp.exp(m_prev - m_new)
            pr = jnp.exp(s - jnp.concatenate([m_new] * nrep, axis=1)).astype(BF16)
            acc_sc[hh] = alpha * acc_sc[hh] + jnp.dot(pr, va, preferred_element_type=F32)
            m_sc[hh] = m_new

    def body(j, carry):
        step(j, False)
        return carry

    lax.fori_loop(0, i // ratio, body, 0)
    step(i // ratio, True)
    outs = []
    for hh in range(2):
        a = acc_sc[hh]
        outs.append(a / pltpu.roll(a, D_HEAD, axis=1))
    o_ref[...] = jnp.where(lane < D_HEAD, outs[0], outs[1]).astype(o_ref.dtype)


def _fox_prompt(q, kb, vb, c, ct, batch, t, tq, tk):
    nq = t // tq
    ct4 = ct.reshape(N_HEADS // 2, 2, batch * t)
    return pl.pallas_call(
        functools.partial(_fox_prompt_kernel, tq=tq, tk=tk),
        grid=(batch, N_HEADS // 2, nq),
        in_specs=[pl.BlockSpec((tq, LANES), lambda b, p, i: (b * nq + i, p)),
                  pl.BlockSpec((t, LANES), lambda b, p, i: (b, p)),
                  pl.BlockSpec((t, LANES), lambda b, p, i: (b, p)),
                  pl.BlockSpec((1, tq, N_HEADS), lambda b, p, i: (b, i, 0)),
                  pl.BlockSpec((1, 2, t), lambda b, p, i: (p, 0, b))],
        out_specs=pl.BlockSpec((tq, LANES), lambda b, p, i: (b * nq + i, p)),
        out_shape=jax.ShapeDtypeStruct((batch * t, D_ATT), BF16),
        scratch_shapes=[pltpu.VMEM((2, tq, LANES), F32), pltpu.VMEM((2, tq, LANES), F32)],
        compiler_params=_params(("parallel", "parallel", "parallel"), 32),
        name="fox_prompt",
    )(q, kb, vb, c, ct4)


def _hgrn_chunk(zq, zf, zi, lb, gw, tri, s_sc, o_sc, *, chunk, sub, n_valid):
    fg = lb + (1.0 - lb) * jax.nn.sigmoid(zf)
    kk = 1.0 - fg
    g = jnp.log(fg)
    if n_valid < chunk:
        valid = lax.broadcasted_iota(jnp.int32, (chunk, D_REC), 0) < n_valid
        kk = jnp.where(valid, kk, 0.0)
        g = jnp.where(valid, g, 0.0)
    q = _silu(zq)
    b = jnp.dot(tri, g, precision=HIGHEST, preferred_element_type=F32)
    b_end = b[chunk - 1:chunk, :]
    qhat = (q * jnp.exp(b)).astype(BF16)
    khat = (kk * jnp.exp(b_end - b)).astype(BF16)
    vb = zi.astype(BF16)
    e_end = jnp.exp(b_end)
    s_old = [s_sc[h] for h in range(N_HEADS)]
    s_old_b = [s.astype(BF16) for s in s_old]
    for i in range(chunk // sub):
        r0, r1 = i * sub, (i + 1) * sub
        rho = b[r0 + sub // 2:r0 + sub // 2 + 1, :]
        qt = (q[r0:r1] * jnp.exp(b[r0:r1] - rho)).astype(BF16)
        kt = (kk[:r1] * jnp.exp(rho - b[:r1])).astype(BF16)
        row = lax.broadcasted_iota(jnp.int32, (sub, r1), 0) + r0
        col = lax.broadcasted_iota(jnp.int32, (sub, r1), 1)
        for h in range(N_HEADS):
            hs = slice(h * D_HEAD, (h + 1) * D_HEAD)
            a = lax.dot_general(qt[:, hs], kt[:, hs], NT_DIMS, preferred_element_type=F32)
            a = jnp.where(col <= row, a, 0.0).astype(BF16)
            o = (jnp.dot(a, vb[:r1, hs], preferred_element_type=F32)
                 + jnp.dot(qhat[r0:r1, hs], s_old_b[h], preferred_element_type=F32))
            ms = jnp.mean(o * o, axis=-1, keepdims=True)
            o_sc[r0:r1, hs] = (o * lax.rsqrt(ms + RMS_EPS)) * gw
    eye = (lax.broadcasted_iota(jnp.int32, (D_HEAD, D_HEAD), 0)
           == lax.broadcasted_iota(jnp.int32, (D_HEAD, D_HEAD), 1))
    for h in range(N_HEADS):
        hs = slice(h * D_HEAD, (h + 1) * D_HEAD)
        e_col = jnp.sum(jnp.where(eye, jnp.broadcast_to(e_end[:, hs], (D_HEAD, D_HEAD)), 0.0), axis=1, keepdims=True)
        s_sc[h] = e_col * s_old[h] + lax.dot_general(khat[:, hs], vb[:, hs], TN_DIMS, preferred_element_type=F32)


def _hgrn_pair(q, kk, b, zi, bd_mean, s_old, *, chunk, sub):
    b_end = b[chunk - 1:chunk, :]
    head0 = lax.broadcasted_iota(jnp.int32, (1, LANES), 1) < D_HEAD
    qhat = (q * jnp.exp(b)).astype(BF16)
    khat = (kk * jnp.exp(b_end - b)).astype(BF16)
    vb = zi.astype(BF16)
    zero_b = jnp.zeros((), BF16)
    v0 = jnp.where(head0, vb, zero_b)
    v1 = jnp.where(head0, zero_b, vb)
    s_old_b = s_old.astype(BF16)
    a0_rows, a1_rows = [], []
    for i in range(chunk // sub):
        r0, r1 = i * sub, (i + 1) * sub
        rho = b[r0 + sub // 2:r0 + sub // 2 + 1, :]
        qt = (q[r0:r1] * jnp.exp(b[r0:r1] - rho)).astype(BF16)
        kt = (kk * jnp.exp(rho - b)).astype(BF16)
        both = jnp.concatenate([jnp.where(head0, qt, zero_b), jnp.where(head0, zero_b, qt)], axis=0)
        a = lax.dot_general(both, kt, NT_DIMS, preferred_element_type=F32)
        a0_rows.append(a[:sub])
        a1_rows.append(a[sub:])
    causal = (lax.broadcasted_iota(jnp.int32, (chunk, chunk), 1)
              <= lax.broadcasted_iota(jnp.int32, (chunk, chunk), 0))
    a0 = jnp.where(causal, jnp.concatenate(a0_rows, axis=0), 0.0).astype(BF16)
    a1 = jnp.where(causal, jnp.concatenate(a1_rows, axis=0), 0.0).astype(BF16)
    o = jnp.dot(jnp.concatenate([a0, a1, qhat], axis=1), jnp.concatenate([v0, v1, s_old_b], axis=0),
                preferred_element_type=F32)
    o2 = o * o
    hi = o2.astype(BF16)
    lo = (o2 - hi.astype(F32)).astype(BF16)
    ms = jnp.dot(jnp.concatenate([hi, lo], axis=1), jnp.concatenate([bd_mean, bd_mean], axis=0),
                 preferred_element_type=F32)
    outs = [o * lax.rsqrt(ms + RMS_EPS)]
    ri = lax.broadcasted_iota(jnp.int32, (LANES, LANES), 0)
    ci = lax.broadcasted_iota(jnp.int32, (LANES, LANES), 1)
    e_col = jnp.sum(jnp.where(ri == ci, jnp.broadcast_to(jnp.exp(b_end), (LANES, LANES)), 0.0), axis=1, keepdims=True)
    upd = lax.dot_general(khat, vb, TN_DIMS, preferred_element_type=F32)
    s_new = e_col * s_old + jnp.where((ri // D_HEAD) == (ci // D_HEAD), upd, 0.0)
    return jnp.concatenate(outs, axis=0), s_new


def _hgrn_prompt_kernel(z_ref, lb_ref, gw_ref, tri_ref, bd_ref, o_ref, s_ref, s_sc, *, chunk, sub, n_chunks):
    j = pl.program_id(1)

    @pl.when(j == 0)
    def _():
        s_sc[...] = jnp.zeros(s_sc.shape, F32)

    s_olds = [s_sc[p] for p in range(N_HEADS // 2)]
    lb = lb_ref[...]
    rows = []
    for c in range(n_chunks):
        z = z_ref[c * chunk:(c + 1) * chunk, :]
        fg = lb + (1.0 - lb) * jax.nn.sigmoid(z[:, D_REC:2 * D_REC])
        kk = 1.0 - fg
        q = _silu(z[:, :D_REC])
        b = sum(jnp.dot(tri_ref[...], part, preferred_element_type=F32) for part in _split3(jnp.log(fg)))
        outs = []
        for p in range(N_HEADS // 2):
            ps = slice(p * LANES, (p + 1) * LANES)
            o, s_new = _hgrn_pair(q[:, ps], kk[:, ps], b[:, ps],
                                  z[:, 2 * D_REC + p * LANES:2 * D_REC + (p + 1) * LANES],
                                  bd_ref[...], s_olds[p], chunk=chunk, sub=sub)
            gate = _silu(z[:, 3 * D_REC + p * LANES:3 * D_REC + (p + 1) * LANES])
            outs.append(((o * gw_ref[...]) * gate).astype(o_ref.dtype))
            s_olds[p] = s_new
        rows.append(jnp.concatenate(outs, axis=1))
    for p in range(N_HEADS // 2):
        s_sc[p] = s_olds[p]
    o_ref[...] = jnp.concatenate(rows, axis=0)

    @pl.when(j == pl.num_programs(1) - 1)
    def _():
        for h in range(N_HEADS):
            d0 = (h % 2) * D_HEAD
            s_ref[0, h] = s_sc[h // 2][d0:d0 + D_HEAD, d0:d0 + D_HEAD]


def _hgrn_prompt(zrec, lb, gw, batch, t, chunk, sub, n_chunks):
    nc = t // (chunk * n_chunks)
    rows = chunk * n_chunks
    idx = jnp.arange(chunk)
    tri = (idx[:, None] >= idx[None, :]).astype(BF16)
    li = jnp.arange(LANES) // D_HEAD
    bd_mean = ((li[:, None] == li[None, :]).astype(F32) / D_HEAD).astype(BF16)
    gw2 = jnp.concatenate([gw, gw], axis=1)
    return pl.pallas_call(
        functools.partial(_hgrn_prompt_kernel, chunk=chunk, sub=sub, n_chunks=n_chunks),
        grid=(batch, nc),
        in_specs=[pl.BlockSpec((rows, 4 * D_REC), lambda b, j: (b * nc + j, 0)),
                  pl.BlockSpec((1, D_REC), lambda b, j: (0, 0)),
                  pl.BlockSpec((1, LANES), lambda b, j: (0, 0)),
                  pl.BlockSpec((chunk, chunk), lambda b, j: (0, 0)),
                  pl.BlockSpec((LANES, LANES), lambda b, j: (0, 0))],
        out_specs=[pl.BlockSpec((rows, D_REC), lambda b, j: (b * nc + j, 0)),
                   pl.BlockSpec((1, N_HEADS, D_HEAD, D_HEAD), lambda b, j: (b, 0, 0, 0))],
        out_shape=[jax.ShapeDtypeStruct((batch * t, D_REC), BF16),
                   jax.ShapeDtypeStruct((batch, N_HEADS, D_HEAD, D_HEAD), F32)],
        scratch_shapes=[pltpu.VMEM((N_HEADS // 2, LANES, LANES), F32)],
        compiler_params=_params(("parallel", "arbitrary"), 32),
        name="hgrn_prompt",
    )(zrec, lb, gw2, tri, bd_mean)


def _hgrn_sample_kernel(z_ref, s0_ref, lb_ref, gw_ref, tri_ref, o_ref, s_ref, s_sc, o_sc, *, seqs, t, chunk):
    lb = lb_ref[...]
    gw = gw_ref[...]
    tri = tri_ref[...]
    pad = jnp.zeros((chunk - t, D_REC), F32)
    for s in range(seqs):
        z = z_ref[s]
        s_sc[...] = s0_ref[s]
        zq, zf, zi = (jnp.concatenate([z[:, u * D_REC:(u + 1) * D_REC], pad], axis=0) for u in range(3))
        _hgrn_chunk(zq, zf, zi, lb, gw, tri, s_sc, o_sc, chunk=chunk, sub=chunk, n_valid=t)
        o_ref[s] = (o_sc[:t, :] * _silu(z[:, 3 * D_REC:])).astype(o_ref.dtype)
        s_ref[s] = s_sc[...]


def _hgrn_sample(zrec, s0, lb, gw, batch, t, seqs):
    chunk = 8
    idx = jnp.arange(chunk)
    tri = (idx[:, None] >= idx[None, :]).astype(F32)
    return pl.pallas_call(
        functools.partial(_hgrn_sample_kernel, seqs=seqs, t=t, chunk=chunk),
        grid=(batch // seqs,),
        in_specs=[pl.BlockSpec((seqs, t, 4 * D_REC), lambda b: (b, 0, 0)),
                  pl.BlockSpec((seqs, N_HEADS, D_HEAD, D_HEAD), lambda b: (b, 0, 0, 0)),
                  pl.BlockSpec((1, D_REC), lambda b: (0, 0)),
                  pl.BlockSpec((1, D_HEAD), lambda b: (0, 0)),
                  pl.BlockSpec((chunk, chunk), lambda b: (0, 0))],
        out_specs=[pl.BlockSpec((seqs, t, D_REC), lambda b: (b, 0, 0)),
                   pl.BlockSpec((seqs, N_HEADS, D_HEAD, D_HEAD), lambda b: (b, 0, 0, 0))],
        out_shape=[jax.ShapeDtypeStruct((batch, t, D_REC), BF16),
                   jax.ShapeDtypeStruct((batch, N_HEADS, D_HEAD, D_HEAD), F32)],
        scratch_shapes=[pltpu.VMEM((N_HEADS, D_HEAD, D_HEAD), F32), pltpu.VMEM((chunk, D_REC), F32)],
        compiler_params=_params(("parallel",), 32),
        name="hgrn_sample",
    )(zrec.reshape(batch, t, 4 * D_REC), s0, lb, gw, tri)


def _fox_sample_kernel(pt_ref, q_ref, kn_ref, vn_ref, lft_ref, gs_ref, ck_ref, cv_ref, clf_ref, o_ref,
                       kbuf, vbuf, lbuf, sem, *, t, ppb, n_pages):
    b = pl.program_id(0)
    nb = pl.num_programs(0)
    nchunk = n_pages // ppb

    def chunk_copies(bb, c, slot):
        for u in range(ppb):
            page = pt_ref[bb, n_pages - 1 - (c * ppb + u)]
            yield pltpu.make_async_copy(ck_ref.at[page], kbuf.at[slot, u], sem.at[0, slot])
            yield pltpu.make_async_copy(cv_ref.at[page], vbuf.at[slot, u], sem.at[1, slot])
            yield pltpu.make_async_copy(clf_ref.at[page], lbuf.at[slot, u], sem.at[2, slot])

    def start_chunk(bb, c, slot):
        for cp in chunk_copies(bb, c, slot):
            cp.start()

    def wait_chunk(slot):
        pltpu.make_async_copy(ck_ref.at[pl.ds(0, ppb)], kbuf.at[slot], sem.at[0, slot]).wait()
        pltpu.make_async_copy(cv_ref.at[pl.ds(0, ppb)], vbuf.at[slot], sem.at[1, slot]).wait()
        pltpu.make_async_copy(clf_ref.at[pl.ds(0, ppb)], lbuf.at[slot], sem.at[2, slot]).wait()

    @pl.when(b == 0)
    def _():
        start_chunk(0, 0, 0)

    nrow = t * N_HEADS
    lane = lax.broadcasted_iota(jnp.int32, (nrow, D_ATT), 1)
    rowi = lax.broadcasted_iota(jnp.int32, (nrow, D_ATT), 0)
    head_lanes = (lane // D_HEAD) == (rowi % N_HEADS)
    q = q_ref[0]
    qbd = jnp.concatenate([jnp.broadcast_to(q[u:u + 1, :], (N_HEADS, D_ATT)) for u in range(t)], axis=0)
    qbd = jnp.where(head_lanes, qbd, jnp.zeros_like(qbd))

    lft = lft_ref[0]
    cols = [lft[:, 0:1]]
    for u in range(1, t):
        cols.append(cols[-1] + lft[:, u:u + 1])
    cq_t = jnp.concatenate(cols, axis=1)
    cq_col = jnp.concatenate(cols, axis=0)
    zpad = jnp.zeros((N_HEADS - t, D_ATT), F32)
    kn = jnp.concatenate([kn_ref[0], zpad], axis=0).astype(BF16)
    vn = jnp.concatenate([vn_ref[0], zpad], axis=0).astype(BF16)
    cq_t8 = jnp.concatenate([cq_t, jnp.zeros((N_HEADS, N_HEADS - t), F32)], axis=1)
    s = lax.dot_general(qbd, kn, NT_DIMS, preferred_element_type=F32)
    s = s + cq_col - jnp.concatenate([cq_t8] * t, axis=0)
    r2 = lax.broadcasted_iota(jnp.int32, (nrow, N_HEADS), 0) // N_HEADS
    c2 = lax.broadcasted_iota(jnp.int32, (nrow, N_HEADS), 1)
    s = jnp.where(c2 <= r2, s, NEG_BIG)
    m = jnp.max(s, axis=-1, keepdims=True)
    p0 = jnp.exp(s - m)
    l = jnp.sum(p0, axis=-1, keepdims=True)
    acc = jnp.dot(p0.astype(BF16), vn, preferred_element_type=F32)
    carry = jnp.zeros((N_HEADS, 1), F32)

    for c in range(nchunk):
        slot = (b * nchunk + c) % 2
        if c + 1 < nchunk:
            start_chunk(b, c + 1, 1 - slot)
        else:
            @pl.when(b + 1 < nb)
            def _():
                start_chunk(b + 1, 0, 1 - slot)
        wait_chunk(slot)
        lf_pages = [lbuf[slot, u] for u in range(ppb)]
        insuf = jnp.dot(jnp.concatenate(lf_pages, axis=0), gs_ref[...], precision=HIGHEST,
                        preferred_element_type=F32)
        biases = []
        for u in range(ppb):
            biases.append(insuf[u * N_HEADS:(u + 1) * N_HEADS, :] + carry)
            carry = carry + jnp.sum(lf_pages[u], axis=-1, keepdims=True)
        bias = jnp.concatenate(biases, axis=1)
        bias = jnp.concatenate([bias] * t, axis=0) + cq_col
        kcat = jnp.concatenate([kbuf[slot, u].astype(BF16) for u in range(ppb)], axis=1)
        vcat = jnp.concatenate([vbuf[slot, u].astype(BF16) for u in range(ppb)], axis=1)
        s = jnp.dot(qbd, kcat, preferred_element_type=F32) + bias
        m_new = jnp.maximum(m, jnp.max(s, axis=-1, keepdims=True))
        alpha = jnp.exp(m - m_new)
        pr = jnp.exp(s - m_new)
        l = alpha * l + jnp.sum(pr, axis=-1, keepdims=True)
        acc = alpha * acc + lax.dot_general(pr.astype(BF16), vcat, NT_DIMS, preferred_element_type=F32)
        m = m_new

    o = jnp.where(head_lanes, acc / l, 0.0)
    o_ref[0] = jnp.sum(o.reshape(t, N_HEADS, D_ATT), axis=1).astype(o_ref.dtype)


def _fox_sample(page_table, q, kn, vn, lft, cache_k, cache_v, cache_lft, batch, t, ppb):
    n_pages = page_table.shape[1]
    ppb = min(ppb, n_pages)
    idx = jnp.arange(PAGE_SIZE)
    gs = (idx[:, None] > idx[None, :]).astype(F32)
    seq = lambda w: pl.BlockSpec((1, t, w), lambda b, pt: (b, 0, 0))
    hbm = pl.BlockSpec(memory_space=pl.ANY)
    grid_spec = pltpu.PrefetchScalarGridSpec(
        num_scalar_prefetch=1,
        grid=(batch,),
        in_specs=[seq(D_ATT), seq(D_ATT), seq(D_ATT),
                  pl.BlockSpec((1, N_HEADS, t), lambda b, pt: (b, 0, 0)),
                  pl.BlockSpec((PAGE_SIZE, PAGE_SIZE), lambda b, pt: (0, 0)),
                  hbm, hbm, hbm],
        out_specs=pl.BlockSpec((1, t, D_ATT), lambda b, pt: (b, 0, 0)),
        scratch_shapes=[pltpu.VMEM((2, ppb, D_ATT, PAGE_SIZE), F32), pltpu.VMEM((2, ppb, D_ATT, PAGE_SIZE), F32),
                        pltpu.VMEM((2, ppb, N_HEADS, PAGE_SIZE), F32), pltpu.SemaphoreType.DMA((3, 2))],
    )
    return pl.pallas_call(
        functools.partial(_fox_sample_kernel, t=t, ppb=ppb, n_pages=n_pages),
        grid_spec=grid_spec,
        out_shape=jax.ShapeDtypeStruct((batch, t, D_ATT), BF16),
        compiler_params=_params(("arbitrary",), 48),
        name="fox_sample",
    )(page_table, q.reshape(batch, t, D_ATT), kn.reshape(batch, t, D_ATT), vn.reshape(batch, t, D_ATT), lft, gs,
      cache_k, cache_v, cache_lft)


def _outproj_router_kernel(x_ref, orec_ref, oatt_ref, wo_ref, nw_ref, wr_ref, br_ref, cnt0_ref, tril_ref,
                           h_ref, xn_ref, idx_ref, gate_ref, rank_ref, cnt_ref, base_sc):
    i = pl.program_id(0)

    @pl.when(i == 0)
    def _():
        base_sc[...] = cnt0_ref[...]

    tm = x_ref.shape[0]
    h = (x_ref[...]
         + jnp.dot(orec_ref[...], wo_ref[:D_REC, :], preferred_element_type=F32)
         + jnp.dot(oatt_ref[...], wo_ref[D_REC:, :], preferred_element_type=F32))
    h_ref[...] = h
    r = lax.rsqrt(jnp.mean(h * h, axis=-1, keepdims=True) + RMS_EPS)
    xn = (h * r) * nw_ref[...]
    xn_ref[...] = xn
    x_hi = xn.astype(BF16)
    x_lo = (xn - x_hi.astype(F32)).astype(BF16)
    logits = (jnp.dot(x_hi, wr_ref[0], preferred_element_type=F32)
              + (jnp.dot(x_lo, wr_ref[0], preferred_element_type=F32)
                 + jnp.dot(x_hi, wr_ref[1], preferred_element_type=F32))) + br_ref[...]
    lane = lax.broadcasted_iota(jnp.int32, (tm, LANES), 1)
    vals, hots = [], []
    idx_out = jnp.zeros((tm, LANES), jnp.int32)
    for k in range(TOP_K):
        m = jnp.max(logits, axis=-1, keepdims=True)
        sel = jnp.min(jnp.where(logits == m, lane, LANES), axis=-1, keepdims=True)
        hot = lane == sel
        vals.append(m)
        hots.append(hot)
        idx_out = jnp.where(lane == k, sel, idx_out)
        logits = jnp.where(hot, -jnp.inf, logits)
    idx_ref[...] = idx_out
    es = [jnp.exp(v - vals[0]) for v in vals]
    denom = es[0] + es[1] + es[2] + es[3]
    gate_out = jnp.zeros((tm, LANES), F32)
    for k in range(TOP_K):
        gate_out = jnp.where(lane == k, es[k] / denom, gate_out)
    gate_ref[...] = gate_out
    base = base_sc[...]
    rank_out = jnp.zeros((tm, LANES), jnp.int32)
    for k in range(TOP_K):
        hot_f = hots[k].astype(F32)
        before = jnp.dot(tril_ref[...], hot_f.astype(BF16), preferred_element_type=F32) + base
        rank = jnp.sum(jnp.where(hots[k], before, 0.0), axis=-1, keepdims=True)
        rank_out = jnp.where(lane == k, rank.astype(jnp.int32), rank_out)
        base = base + jnp.sum(hot_f, axis=0, keepdims=True)
    rank_ref[...] = rank_out
    base_sc[...] = base
    cnt_ref[...] = base


def _outproj_router(x, orec, oatt, wo, nw, wr, br, cnt0, tm):
    n = x.shape[0]
    idx = jnp.arange(tm)
    tril = (idx[:, None] > idx[None, :]).astype(BF16)
    full = lambda a: pl.BlockSpec(a.shape, lambda i: (0,) * a.ndim)
    row = lambda w: pl.BlockSpec((tm, w), lambda i: (i, 0))
    return pl.pallas_call(
        _outproj_router_kernel,
        grid=(n // tm,),
        in_specs=[row(D_MODEL), row(D_REC), row(D_ATT), full(wo), full(nw), full(wr), full(br), full(cnt0), full(tril)],
        out_specs=[row(D_MODEL), row(D_MODEL), row(LANES), row(LANES), row(LANES),
                   pl.BlockSpec((1, LANES), lambda i: (0, 0))],
        out_shape=[jax.ShapeDtypeStruct((n, D_MODEL), F32), jax.ShapeDtypeStruct((n, D_MODEL), F32),
                   jax.ShapeDtypeStruct((n, LANES), jnp.int32), jax.ShapeDtypeStruct((n, LANES), F32),
                   jax.ShapeDtypeStruct((n, LANES), jnp.int32), jax.ShapeDtypeStruct((1, LANES), F32)],
        scratch_shapes=[pltpu.VMEM((1, LANES), F32)],
        compiler_params=_params(("arbitrary",), 40),
        name="outproj_router",
    )(x, orec, oatt, wo, nw, wr, br, cnt0, tril)


def _dispatch_kernel(gend_ref, gsize_ref, total_ref, dest_ref, xp_ref, xs_ref, out_ref, zbuf, sem, zsem,
                     *, n_prompt_tiles, tile_rows):
    i = pl.program_id(0)
    n_tiles = out_ref.shape[0] // tile_rows

    @pl.when(i == 0)
    def _():
        zbuf[...] = jnp.zeros(zbuf.shape, zbuf.dtype)

        def zero_tile(start):
            cp = pltpu.make_async_copy(zbuf, out_ref.at[pl.ds(pl.multiple_of(start, tile_rows), tile_rows), :], zsem)
            cp.start()
            cp.wait()

        for e in range(N_EXPERTS):
            @pl.when(gsize_ref[e] > 0)
            def _():
                zero_tile(gend_ref[e] - tile_rows)

        def tail(tt, carry):
            zero_tile(tt * tile_rows)
            return carry

        lax.fori_loop(total_ref[0] // tile_rows, n_tiles, tail, 0)

    def scatter(src_ref):
        n = src_ref.shape[0] * SUBLANES

        def issue(g, carry):
            for u in range(SUBLANES):
                for k in range(TOP_K):
                    pltpu.make_async_copy(src_ref.at[g, pl.ds(u, 1), :],
                                          out_ref.at[pl.ds(dest_ref[0, 0, (g * SUBLANES + u) * TOP_K + k], 1), :],
                                          sem).start(priority=k % 2)
            return carry

        lax.fori_loop(0, n // SUBLANES, issue, 0)
        for k in range(TOP_K):
            pltpu.make_async_copy(zbuf.at[pl.ds(0, n), :], out_ref.at[pl.ds(0, n), :], sem).wait()

    @pl.when(i < n_prompt_tiles)
    def _():
        scatter(xp_ref)

    @pl.when(i == n_prompt_tiles)
    def _():
        scatter(xs_ref)


def _dispatch(group_end, group_size, dest_p, dest_s, xn_p, xn_s, n_rows, tm, tile_rows):
    n_p, n_s = xn_p.shape[0], xn_s.shape[0]
    npt = n_p // tm
    dest = jnp.concatenate([dest_p.reshape(npt, tm * TOP_K),
                            jnp.pad(dest_s, (0, (tm - n_s) * TOP_K)).reshape(1, tm * TOP_K)], axis=0)
    grid_spec = pltpu.PrefetchScalarGridSpec(
        num_scalar_prefetch=3,
        grid=(npt + 1,),
        in_specs=[pl.BlockSpec((1, 1, tm * TOP_K), lambda i, *_: (i, 0, 0), memory_space=pltpu.SMEM),
                  pl.BlockSpec((tm // SUBLANES, SUBLANES, D_MODEL), lambda i, *_: (jnp.minimum(i, npt - 1), 0, 0)),
                  pl.BlockSpec((n_s // SUBLANES, SUBLANES, D_MODEL), lambda i, *_: (0, 0, 0))],
        out_specs=pl.BlockSpec(memory_space=pl.ANY),
        scratch_shapes=[pltpu.VMEM((tile_rows, D_MODEL), F32), pltpu.SemaphoreType.DMA(()),
                        pltpu.SemaphoreType.DMA(())],
    )
    return pl.pallas_call(
        functools.partial(_dispatch_kernel, n_prompt_tiles=npt, tile_rows=tile_rows),
        grid_spec=grid_spec,
        out_shape=jax.ShapeDtypeStruct((n_rows, D_MODEL), F32),
        compiler_params=_params(("arbitrary",), 32),
        name="moe_dispatch",
    )(group_end, group_size, group_end[-1:], dest.reshape(npt + 1, 1, tm * TOP_K),
      xn_p.reshape(n_p // SUBLANES, SUBLANES, D_MODEL), xn_s.reshape(n_s // SUBLANES, SUBLANES, D_MODEL))


def _expert_mlp_kernel(te_ref, tv_ref, tf_ref, x_ref, wg_ref, bg_ref, wu_ref, bu_ref, wd_ref, bd_ref, y_ref,
                       wg_sc, wu_sc, wd_sc):
    i = pl.program_id(0)
    del te_ref

    @pl.when(tf_ref[i] != 0)
    def _():
        wg_sc[...] = wg_ref[0].astype(BF16)
        wu_sc[...] = wu_ref[0].astype(BF16)
        wd_sc[...] = wd_ref[0].astype(BF16)

    @pl.when(tv_ref[i] != 0)
    def _():
        x = x_ref[...].astype(BF16)
        gate = jnp.minimum(jnp.dot(x, wg_sc[...], preferred_element_type=F32) + bg_ref[0], SWIGLU_LIMIT)
        up = jnp.clip(jnp.dot(x, wu_sc[...], preferred_element_type=F32) + bu_ref[0], -SWIGLU_LIMIT, SWIGLU_LIMIT)
        glu = gate * jax.nn.sigmoid(gate * SWIGLU_ALPHA)
        mid = ((up + 1.0) * glu).astype(BF16)
        y_ref[...] = jnp.dot(mid, wd_sc[...], preferred_element_type=F32) + bd_ref[0]

    @pl.when(tv_ref[i] == 0)
    def _():
        y_ref[...] = jnp.zeros(y_ref.shape, y_ref.dtype)


def _expert_mlp(tile_expert, tile_valid, tile_first, xs, wg, bg, wu, bu, wd, bd, tm):
    p = xs.shape[0]
    wspec = lambda: pl.BlockSpec((1, D_MODEL, D_FF), lambda i, te, tv, tf: (te[i], 0, 0))
    bspec = lambda: pl.BlockSpec((1, 1, D_FF), lambda i, te, tv, tf: (te[i], 0, 0))
    grid_spec = pltpu.PrefetchScalarGridSpec(
        num_scalar_prefetch=3,
        grid=(p // tm,),
        in_specs=[pl.BlockSpec((tm, D_MODEL), lambda i, te, tv, tf: (i * tv[i], 0)),
                  wspec(), bspec(), wspec(), bspec(), wspec(), bspec()],
        out_specs=pl.BlockSpec((tm, D_MODEL), lambda i, te, tv, tf: (i, 0)),
        scratch_shapes=[pltpu.VMEM((D_MODEL, D_FF), BF16), pltpu.VMEM((D_MODEL, D_FF), BF16),
                        pltpu.VMEM((D_FF, D_MODEL), BF16)],
    )
    return pl.pallas_call(
        _expert_mlp_kernel,
        grid_spec=grid_spec,
        out_shape=jax.ShapeDtypeStruct((p, D_MODEL), F32),
        compiler_params=_params(("arbitrary",), 56),
        name="expert_mlp",
    )(tile_expert, tile_valid, tile_first, xs, wg, bg, wu, bu, wd, bd)


def _combine_kernel(dest_ref, dest_next_ref, gate_ref, h_ref, nw_ref, ys_ref, y_ref, buf, sem):
    i = pl.program_id(0)
    tm = h_ref.shape[0]
    slot = i % 2

    def request(dref, s):
        def issue(g, carry):
            for u in range(SUBLANES):
                for k in range(TOP_K):
                    pltpu.make_async_copy(ys_ref.at[pl.ds(dref[0, 0, (g * SUBLANES + u) * TOP_K + k], 1), :],
                                          buf.at[s, k, g, pl.ds(u, 1), :], sem.at[s]).start(priority=k % 2)
            return carry

        lax.fori_loop(0, tm // SUBLANES, issue, 0)

    @pl.when(i == 0)
    def _():
        request(dest_ref, 0)

    for s in range(2):
        @pl.when(slot == s)
        def _():
            @pl.when(i + 1 < pl.num_programs(0))
            def _():
                request(dest_next_ref, 1 - s)

            for k in range(TOP_K):
                pltpu.make_async_copy(ys_ref.at[pl.ds(0, tm), :], y_ref, sem.at[s]).wait()

    gates = gate_ref[...]
    moe = gates[:, 0:1] * buf[slot, 0].reshape(tm, D_MODEL)
    for k in range(1, TOP_K):
        moe = moe + gates[:, k:k + 1] * buf[slot, k].reshape(tm, D_MODEL)
    h = h_ref[...] + moe
    r = lax.rsqrt(jnp.mean(h * h, axis=-1, keepdims=True) + RMS_EPS)
    y_ref[...] = (h * r) * nw_ref[...]


def _combine(dest, gates, h, nw, ys, tm):
    n = h.shape[0]
    nt = n // tm
    dest3 = dest.reshape(nt, 1, tm * TOP_K)
    return pl.pallas_call(
        _combine_kernel,
        grid=(nt,),
        in_specs=[pl.BlockSpec((1, 1, tm * TOP_K), lambda i: (i, 0, 0), memory_space=pltpu.SMEM),
                  pl.BlockSpec((1, 1, tm * TOP_K), lambda i: (jnp.minimum(i + 1, nt - 1), 0, 0),
                               memory_space=pltpu.SMEM),
                  pl.BlockSpec((tm, LANES), lambda i: (i, 0)),
                  pl.BlockSpec((tm, D_MODEL), lambda i: (i, 0)),
                  pl.BlockSpec((1, D_MODEL), lambda i: (0, 0)),
                  pl.BlockSpec(memory_space=pl.ANY)],
        out_specs=pl.BlockSpec((tm, D_MODEL), lambda i: (i, 0)),
        out_shape=jax.ShapeDtypeStruct((n, D_MODEL), F32),
        scratch_shapes=[pltpu.VMEM((2, TOP_K, tm // SUBLANES, SUBLANES, D_MODEL), F32),
                        pltpu.SemaphoreType.DMA((2,))],
        compiler_params=_params(("arbitrary",), 32),
        name="moe_combine",
    )(dest3, dest3, gates, h, nw, ys)


TM_TOKENS = 512
TM_EXPERT = 512
TM_COMBINE = 256
TQ_PROMPT = 512
TK_PROMPT = 512
HGRN_CHUNK = 128
HGRN_SUB = 32
HGRN_CHUNKS_PER_STEP = 4
PAGES_PER_STEP = 16
HGRN_SAMPLE_SEQS = 8


def _step(x_prompt, x_sample, cache_k, cache_v, cache_logf, state_hgrn, page_table,
          norm_mix_w, w_in, b_fa, lb_param, gnorm_w, w_out, norm_ffn_w,
          w_router, b_router, w_gate, b_gate, w_up, b_up, w_down, b_down, norm_final_w):
    batch, t, _ = x_prompt.shape
    dec_batch, dec_t, _ = x_sample.shape
    n_p, n_s = batch * t, dec_batch * dec_t
    n_pool = cache_k.shape[1]

    w_in_b = w_in[0].astype(BF16)
    wrec = w_in_b[:, :4 * D_REC]
    watt = w_in_b[:, 4 * D_REC:4 * D_REC + 3 * D_ATT]
    wfa = jnp.pad(w_in_b[:, 4 * D_REC + 3 * D_ATT:], ((0, 0), (0, LANES - N_HEADS)))
    wfat = jnp.pad(w_in_b[:, 4 * D_REC + 3 * D_ATT:].T, ((0, 16 - N_HEADS), (0, 0)))
    bfa = b_fa[0].reshape(1, N_HEADS)
    bfat = b_fa[0].reshape(N_HEADS, 1)
    nw_mix = norm_mix_w[0].reshape(1, D_MODEL)
    nw_ffn = norm_ffn_w[0].reshape(1, D_MODEL)
    nw_fin = norm_final_w.reshape(1, D_MODEL)
    lb = jnp.cumsum(jax.nn.softmax(lb_param.astype(F32), axis=0), axis=0)[0].reshape(1, D_REC)
    gw = gnorm_w.reshape(1, D_HEAD)
    wo = w_out[0].astype(BF16)
    wr_f = jnp.pad(w_router[0], ((0, 0), (0, LANES - N_EXPERTS)))
    wr_hi = wr_f.astype(BF16)
    wr = jnp.stack([wr_hi, (wr_f - wr_hi.astype(F32)).astype(BF16)])
    br = jnp.pad(b_router[0], (0, LANES - N_EXPERTS), constant_values=NEG_BIG).reshape(1, LANES)
    wg, wu, wd = w_gate[0], w_up[0], w_down[0]
    bg, bu, bd = (b[0].reshape(N_EXPERTS, 1, -1) for b in (b_gate, b_up, b_down))

    xp = x_prompt.reshape(n_p, D_MODEL)
    xs = x_sample.reshape(n_s, D_MODEL)
    tm_p, tm_s = min(TM_TOKENS, n_p), min(TM_TOKENS, n_s)
    proj = functools.partial(_inproj, nw=nw_mix, wrec=wrec, watt=watt, wfa=wfa, wfat=wfat, bfa=bfa, bfat=bfat)
    zrec_p, q_p, kt_p, vt_p, kb_p, vb_p, lf_p, lft_p = proj(xp, tm=tm_p, seq_len=t)
    zrec_s, q_s, k_s, v_s, _, _, lf_s, lft_s = proj(xs, tm=tm_s)

    c_p, ct_p = _cumsum_logf(lf_p, lft_p, batch, t)
    oatt_p = _fox_prompt(q_p, kb_p, vb_p, c_p, ct_p, batch, t, TQ_PROMPT, TK_PROMPT)
    orec_p, state_p = _hgrn_prompt(zrec_p, lb, gw, batch, t, HGRN_CHUNK, HGRN_SUB, HGRN_CHUNKS_PER_STEP)

    orec_s, state_s = _hgrn_sample(zrec_s, state_hgrn[0], lb, gw, dec_batch, dec_t, HGRN_SAMPLE_SEQS)
    lft_s3 = lft_s.reshape(N_HEADS, dec_batch, dec_t).transpose(1, 0, 2)
    cache_lft = cache_logf[0].transpose(0, 2, 1)
    oatt_s = _fox_sample(page_table, q_s, k_s, v_s, lft_s3,
                         cache_k[0].transpose(0, 2, 3, 1).reshape(n_pool, D_ATT, PAGE_SIZE),
                         cache_v[0].transpose(0, 2, 3, 1).reshape(n_pool, D_ATT, PAGE_SIZE),
                         cache_lft, dec_batch, dec_t, PAGES_PER_STEP)

    route = functools.partial(_outproj_router, wo=wo, nw=nw_ffn, wr=wr, br=br)
    h_p, xn_p, idx_p, gate_p, rank_p, cnt_p = route(xp, orec_p, oatt_p, cnt0=jnp.zeros((1, LANES), F32), tm=tm_p)
    h_s, xn_s, idx_s, gate_s, rank_s, cnt = route(xs, orec_s.reshape(n_s, D_REC), oatt_s.reshape(n_s, D_ATT),
                                                  cnt0=cnt_p, tm=tm_s)

    counts = cnt[0, :N_EXPERTS].astype(jnp.int32)
    padded = ((counts + TM_EXPERT - 1) // TM_EXPERT) * TM_EXPERT
    group_end = jnp.cumsum(padded)
    group_start = group_end - padded
    n_rows = (n_p + n_s) * TOP_K + N_EXPERTS * TM_EXPERT
    tile_start = jnp.arange(n_rows // TM_EXPERT, dtype=jnp.int32) * TM_EXPERT
    tile_expert = jnp.minimum(jnp.sum((tile_start[:, None] >= group_end[None, :]).astype(jnp.int32), axis=1),
                              N_EXPERTS - 1)
    tile_valid = (tile_start < group_end[-1]).astype(jnp.int32)
    tile_first = jnp.concatenate([jnp.ones((1,), jnp.int32),
                                  (tile_expert[1:] != tile_expert[:-1]).astype(jnp.int32)])
    dest_p = (group_start[idx_p[:, :TOP_K]] + rank_p[:, :TOP_K]).reshape(-1)
    dest_s = (group_start[idx_s[:, :TOP_K]] + rank_s[:, :TOP_K]).reshape(-1)

    x_sorted = _dispatch(group_end.astype(jnp.int32), padded, dest_p, dest_s, xn_p, xn_s, n_rows, tm_p, TM_EXPERT)
    y_sorted = _expert_mlp(tile_expert, tile_valid, tile_first, x_sorted, wg, bg, wu, bu, wd, bd, TM_EXPERT)
    y_p = _combine(dest_p, gate_p, h_p, nw_fin, y_sorted, min(TM_COMBINE, n_p))
    y_s = _combine(dest_s, gate_s, h_s, nw_fin, y_sorted, n_s)

    heads = lambda a, b_, t_: a.reshape(1, b_, t_, N_HEADS, D_HEAD)
    heads_t = lambda a: a.reshape(1, batch, N_HEADS, D_HEAD, t).transpose(0, 1, 4, 2, 3)
    return (y_p.reshape(batch, t, D_MODEL), y_s.reshape(dec_batch, dec_t, D_MODEL),
            heads_t(kt_p), heads_t(vt_p), lf_p.reshape(1, batch, t, N_HEADS), state_p[None],
            heads(k_s, dec_batch, dec_t), heads(v_s, dec_batch, dec_t), lf_s.reshape(1, dec_batch, dec_t, N_HEADS),
            state_s[None])


def kernel(x_prompt, x_sample, cache_k, cache_v, cache_logf, state_hgrn, page_table, norm_mix_w, w_in, b_fa, lb_param, gnorm_w, w_out, norm_ffn_w, w_router, b_router, w_gate, b_gate, w_up, b_up, w_down, b_down, norm_final_w):
    return _step(x_prompt, x_sample, cache_k, cache_v, cache_logf, state_hgrn, page_table,
                 norm_mix_w, w_in, b_fa, lb_param, gnorm_w, w_out, norm_ffn_w,
                 w_router, b_router, w_gate, b_gate, w_up, b_up, w_down, b_down, norm_final_w)
```

```python
import functools

import jax
import jax.numpy as jnp
from jax import lax
from jax.experimental import pallas as pl
from jax.experimental.pallas import tpu as pltpu

D_MODEL = 1024
D_REC = 512
D_ATT = 512
N_HEADS = 8
D_HEAD = 64
N_EXPERTS = 32
TOP_K = 4
D_FF = 1024
PAGE_SIZE = 128
SWIGLU_LIMIT = 7.0
SWIGLU_ALPHA = 1.702
RMS_EPS = 1e-5

LANES = 128
SUBLANES = 8
NEG_BIG = -1e30
MIB = 1024 * 1024

F32 = jnp.float32
BF16 = jnp.bfloat16
HIGHEST = lax.Precision.HIGHEST

NT_DIMS = (((1,), (1,)), ((), ()))
TN_DIMS = (((0,), (0,)), ((), ()))


def _log_sigmoid(x):
    return jnp.minimum(x, 0.0) - jnp.log1p(jnp.exp(-jnp.abs(x)))


def _silu(x):
    return x * jax.nn.sigmoid(x)


def _split3(x):
    parts = []
    for _ in range(3):
        part = x.astype(BF16)
        parts.append(part)
        x = x - part.astype(F32)
    return parts


def _params(sem, vmem_mib):
    return pltpu.CompilerParams(dimension_semantics=sem, vmem_limit_bytes=vmem_mib * MIB)


def _inproj_kernel(x_ref, nw_ref, wrec_ref, watt_ref, wfa_ref, wfat_ref, bfa_ref, bfat_ref,
                   zrec_ref, q_ref, k_ref, v_ref, kb_ref, vb_ref, lf_ref, lft_ref, *, kv_transposed):
    x = x_ref[...]
    r = lax.rsqrt(jnp.mean(x * x, axis=-1, keepdims=True) + RMS_EPS)
    xn = ((x * r) * nw_ref[...]).astype(BF16)
    zrec_ref[...] = jnp.dot(xn, wrec_ref[...], preferred_element_type=F32)
    za = jnp.dot(xn, watt_ref[...], preferred_element_type=F32)
    q_ref[...] = (za[:, :D_ATT] * (D_HEAD ** -0.5)).astype(BF16)
    k = za[:, D_ATT:2 * D_ATT]
    v = za[:, 2 * D_ATT:]
    if kv_transposed:
        k_ref[0] = k.T
        v_ref[0] = v.T
    else:
        k_ref[...] = k
        v_ref[...] = v
    kb_ref[...] = k.astype(BF16)
    vb_ref[...] = v.astype(BF16)
    fa = jnp.dot(xn, wfa_ref[...], preferred_element_type=F32)
    lf_ref[...] = _log_sigmoid(fa[:, :N_HEADS] + bfa_ref[...])
    fat = lax.dot_general(wfat_ref[...], xn, NT_DIMS, preferred_element_type=F32)
    lft_ref[...] = _log_sigmoid(fat[:N_HEADS, :] + bfat_ref[...])


def _inproj(x, nw, wrec, watt, wfa, wfat, bfa, bfat, tm, seq_len=None):
    n = x.shape[0]
    full = lambda a: pl.BlockSpec(a.shape, lambda i: (0,) * a.ndim)
    row = lambda w: pl.BlockSpec((tm, w), lambda i: (i, 0))
    if seq_len is None:
        kv_spec, kv_shape = row(D_ATT), jax.ShapeDtypeStruct((n, D_ATT), F32)
    else:
        nt = seq_len // tm
        kv_spec = pl.BlockSpec((1, D_ATT, tm), lambda i: (i // nt, 0, i % nt))
        kv_shape = jax.ShapeDtypeStruct((n // seq_len, D_ATT, seq_len), F32)
    return pl.pallas_call(
        functools.partial(_inproj_kernel, kv_transposed=seq_len is not None),
        grid=(n // tm,),
        in_specs=[row(D_MODEL), full(nw), full(wrec), full(watt), full(wfa), full(wfat), full(bfa), full(bfat)],
        out_specs=[row(4 * D_REC), row(D_ATT), kv_spec, kv_spec, row(D_ATT), row(D_ATT), row(N_HEADS),
                   pl.BlockSpec((N_HEADS, tm), lambda i: (0, i))],
        out_shape=[jax.ShapeDtypeStruct((n, 4 * D_REC), F32), jax.ShapeDtypeStruct((n, D_ATT), BF16),
                   kv_shape, kv_shape,
                   jax.ShapeDtypeStruct((n, D_ATT), BF16), jax.ShapeDtypeStruct((n, D_ATT), BF16),
                   jax.ShapeDtypeStruct((n, N_HEADS), F32), jax.ShapeDtypeStruct((N_HEADS, n), F32)],
        compiler_params=_params(("parallel",), 52),
        name="inproj",
    )(x, nw, wrec, watt, wfa, wfat, bfa, bfat)


def _cumsum_kernel(lf_ref, lft_ref, tri_ref, triu_ref, c_ref, ct_ref, *, tb):
    t = lf_ref.shape[1]
    tri = tri_ref[...]
    triu = triu_ref[...]
    carry = jnp.zeros((1, N_HEADS), F32)
    carry_t = jnp.zeros((N_HEADS, 1), F32)
    for j in range(t // tb):
        sl = slice(j * tb, (j + 1) * tb)
        cb = sum(jnp.dot(tri, part, preferred_element_type=F32) for part in _split3(lf_ref[0, sl, :])) + carry
        c_ref[0, sl, :] = cb
        carry = cb[tb - 1:tb, :]
        cbt = sum(jnp.dot(part, triu, preferred_element_type=F32) for part in _split3(lft_ref[:, sl])) + carry_t
        ct_ref[:, sl] = cbt
        carry_t = cbt[:, tb - 1:tb]


def _cumsum_logf(lf, lft, batch, t):
    tb = min(t, 512)
    idx = jnp.arange(tb)
    tri = (idx[:, None] >= idx[None, :]).astype(BF16)
    return pl.pallas_call(
        functools.partial(_cumsum_kernel, tb=tb),
        grid=(batch,),
        in_specs=[pl.BlockSpec((1, t, N_HEADS), lambda b: (b, 0, 0)),
                  pl.BlockSpec((N_HEADS, t), lambda b: (0, b)),
                  pl.BlockSpec((tb, tb), lambda b: (0, 0)),
                  pl.BlockSpec((tb, tb), lambda b: (0, 0))],
        out_specs=[pl.BlockSpec((1, t, N_HEADS), lambda b: (b, 0, 0)),
                   pl.BlockSpec((N_HEADS, t), lambda b: (0, b))],
        out_shape=[jax.ShapeDtypeStruct((batch, t, N_HEADS), F32),
                   jax.ShapeDtypeStruct((N_HEADS, batch * t), F32)],
        compiler_params=_params(("parallel",), 32),
        name="cumsum_logf",
    )(lf.reshape(batch, t, N_HEADS), lft, tri, tri.T)


def _fox_prompt_kernel(q_ref, k_ref, v_ref, c_ref, ct_ref, o_ref, m_sc, acc_sc, *, tq, tk):
    p = pl.program_id(1)
    i = pl.program_id(2)
    ratio = tk // tq
    nrep = tk // LANES
    q = q_ref[...]
    lane = lax.broadcasted_iota(jnp.int32, (tq, LANES), 1)
    lane8 = lax.broadcasted_iota(jnp.int32, (tq, N_HEADS), 1)
    lane_k = lax.broadcasted_iota(jnp.int32, (tk, LANES), 1)
    cblk = c_ref[0]
    qs, cqs = [], []
    for hh in range(2):
        qs.append(jnp.where((lane // D_HEAD) == hh, q, jnp.zeros_like(q)))
        cq = jnp.sum(jnp.where(lane8 == 2 * p + hh, cblk, 0.0), axis=-1, keepdims=True)
        cqs.append(jnp.concatenate([jnp.broadcast_to(cq, (tq, LANES))] * nrep, axis=1))
    m_sc[...] = jnp.full(m_sc.shape, NEG_BIG, F32)
    acc_sc[...] = jnp.zeros(acc_sc.shape, F32)

    def step(j, masked):
        off = pl.multiple_of(j * tk, tk)
        kb = k_ref[pl.ds(off, tk), :]
        vb = v_ref[pl.ds(off, tk), :]
        for hh in range(2):
            va = jnp.where((lane_k // D_HEAD) == hh, vb, jnp.ones_like(vb))
            s = lax.dot_general(qs[hh], kb, NT_DIMS, preferred_element_type=F32)
            s = (s + cqs[hh]) - ct_ref[0, hh:hh + 1, pl.ds(off, tk)]
            if masked:
                row = lax.broadcasted_iota(jnp.int32, (tq, tk), 0) + (i % ratio) * tq
                col = lax.broadcasted_iota(jnp.int32, (tq, tk), 1)
                s = jnp.where(col <= row, s, NEG_BIG)
            m_prev = m_sc[hh]
            m_new = jnp.maximum(m_prev, jnp.max(s, axis=-1, keepdims=True))
            alpha = jnp.exp(m_prev - m_new)
            pr = jnp.exp(s - jnp.concatenate([m_new] * nrep, axis=1)).astype(BF16)
            acc_sc[hh] = alpha * acc_sc[hh] + jnp.dot(pr, va, preferred_element_type=F32)
            m_sc[hh] = m_new

    def body(j, carry):
        step(j, False)
        return carry

    lax.fori_loop(0, i // ratio, body, 0)
    step(i // ratio, True)
    outs = []
    for hh in range(2):
        a = acc_sc[hh]
        outs.append(a / pltpu.roll(a, D_HEAD, axis=1))
    o_ref[...] = jnp.where(lane < D_HEAD, outs[0], outs[1]).astype(o_ref.dtype)


def _fox_prompt(q, kb, vb, c, ct, batch, t, tq, tk):
    nq = t // tq
    ct4 = ct.reshape(N_HEADS // 2, 2, batch * t)
    return pl.pallas_call(
        functools.partial(_fox_prompt_kernel, tq=tq, tk=tk),
        grid=(batch, N_HEADS // 2, nq),
        in_specs=[pl.BlockSpec((tq, LANES), lambda b, p, i: (b * nq + i, p)),
                  pl.BlockSpec((t, LANES), lambda b, p, i: (b, p)),
                  pl.BlockSpec((t, LANES), lambda b, p, i: (b, p)),
                  pl.BlockSpec((1, tq, N_HEADS), lambda b, p, i: (b, i, 0)),
                  pl.BlockSpec((1, 2, t), lambda b, p, i: (p, 0, b))],
        out_specs=pl.BlockSpec((tq, LANES), lambda b, p, i: (b * nq + i, p)),
        out_shape=jax.ShapeDtypeStruct((batch * t, D_ATT), BF16),
        scratch_shapes=[pltpu.VMEM((2, tq, LANES), F32), pltpu.VMEM((2, tq, LANES), F32)],
        compiler_params=_params(("parallel", "parallel", "parallel"), 32),
        name="fox_prompt",
    )(q, kb, vb, c, ct4)


def _hgrn_chunk(zq, zf, zi, lb, gw, tri, s_sc, o_sc, *, chunk, sub, n_valid):
    fg = lb + (1.0 - lb) * jax.nn.sigmoid(zf)
    kk = 1.0 - fg
    g = jnp.log(fg)
    if n_valid < chunk:
        valid = lax.broadcasted_iota(jnp.int32, (chunk, D_REC), 0) < n_valid
        kk = jnp.where(valid, kk, 0.0)
        g = jnp.where(valid, g, 0.0)
    q = _silu(zq)
    b = jnp.dot(tri, g, precision=HIGHEST, preferred_element_type=F32)
    b_end = b[chunk - 1:chunk, :]
    qhat = (q * jnp.exp(b)).astype(BF16)
    khat = (kk * jnp.exp(b_end - b)).astype(BF16)
    vb = zi.astype(BF16)
    e_end = jnp.exp(b_end)
    s_old = [s_sc[h] for h in range(N_HEADS)]
    s_old_b = [s.astype(BF16) for s in s_old]
    for i in range(chunk // sub):
        r0, r1 = i * sub, (i + 1) * sub
        rho = b[r0 + sub // 2:r0 + sub // 2 + 1, :]
        qt = (q[r0:r1] * jnp.exp(b[r0:r1] - rho)).astype(BF16)
        kt = (kk[:r1] * jnp.exp(rho - b[:r1])).astype(BF16)
        row = lax.broadcasted_iota(jnp.int32, (sub, r1), 0) + r0
        col = lax.broadcasted_iota(jnp.int32, (sub, r1), 1)
        for h in range(N_HEADS):
            hs = slice(h * D_HEAD, (h + 1) * D_HEAD)
            a = lax.dot_general(qt[:, hs], kt[:, hs], NT_DIMS, preferred_element_type=F32)
            a = jnp.where(col <= row, a, 0.0).astype(BF16)
            o = (jnp.dot(a, vb[:r1, hs], preferred_element_type=F32)
                 + jnp.dot(qhat[r0:r1, hs], s_old_b[h], preferred_element_type=F32))
            ms = jnp.mean(o * o, axis=-1, keepdims=True)
            o_sc[r0:r1, hs] = (o * lax.rsqrt(ms + RMS_EPS)) * gw
    eye = (lax.broadcasted_iota(jnp.int32, (D_HEAD, D_HEAD), 0)
           == lax.broadcasted_iota(jnp.int32, (D_HEAD, D_HEAD), 1))
    for h in range(N_HEADS):
        hs = slice(h * D_HEAD, (h + 1) * D_HEAD)
        e_col = jnp.sum(jnp.where(eye, jnp.broadcast_to(e_end[:, hs], (D_HEAD, D_HEAD)), 0.0), axis=1, keepdims=True)
        s_sc[h] = e_col * s_old[h] + lax.dot_general(khat[:, hs], vb[:, hs], TN_DIMS, preferred_element_type=F32)


def _hgrn_pair(q, kk, b, zi, bd_mean, s_old, *, chunk, sub):
    b_end = b[chunk - 1:chunk, :]
    head0 = lax.broadcasted_iota(jnp.int32, (1, LANES), 1) < D_HEAD
    qhat = (q * jnp.exp(b)).astype(BF16)
    khat = (kk * jnp.exp(b_end - b)).astype(BF16)
    vb = zi.astype(BF16)
    zero_b = jnp.zeros((), BF16)
    v0 = jnp.where(head0, vb, zero_b)
    v1 = jnp.where(head0, zero_b, vb)
    s_old_b = s_old.astype(BF16)
    a0_rows, a1_rows = [], []
    for i in range(chunk // sub):
        r0, r1 = i * sub, (i + 1) * sub
        rho = b[r0 + sub // 2:r0 + sub // 2 + 1, :]
        qt = (q[r0:r1] * jnp.exp(b[r0:r1] - rho)).astype(BF16)
        kt = (kk * jnp.exp(rho - b)).astype(BF16)
        both = jnp.concatenate([jnp.where(head0, qt, zero_b), jnp.where(head0, zero_b, qt)], axis=0)
        a = lax.dot_general(both, kt, NT_DIMS, preferred_element_type=F32)
        a0_rows.append(a[:sub])
        a1_rows.append(a[sub:])
    causal = (lax.broadcasted_iota(jnp.int32, (chunk, chunk), 1)
              <= lax.broadcasted_iota(jnp.int32, (chunk, chunk), 0))
    a0 = jnp.where(causal, jnp.concatenate(a0_rows, axis=0), 0.0).astype(BF16)
    a1 = jnp.where(causal, jnp.concatenate(a1_rows, axis=0), 0.0).astype(BF16)
    o = jnp.dot(jnp.concatenate([a0, a1, qhat], axis=1), jnp.concatenate([v0, v1, s_old_b], axis=0),
                preferred_element_type=F32)
    o2 = o * o
    hi = o2.astype(BF16)
    lo = (o2 - hi.astype(F32)).astype(BF16)
    ms = jnp.dot(jnp.concatenate([hi, lo], axis=1), jnp.concatenate([bd_mean, bd_mean], axis=0),
                 preferred_element_type=F32)
    outs = [o * lax.rsqrt(ms + RMS_EPS)]
    ri = lax.broadcasted_iota(jnp.int32, (LANES, LANES), 0)
    ci = lax.broadcasted_iota(jnp.int32, (LANES, LANES), 1)
    e_col = jnp.sum(jnp.where(ri == ci, jnp.broadcast_to(jnp.exp(b_end), (LANES, LANES)), 0.0), axis=1, keepdims=True)
    upd = lax.dot_general(khat, vb, TN_DIMS, preferred_element_type=F32)
    s_new = e_col * s_old + jnp.where((ri // D_HEAD) == (ci // D_HEAD), upd, 0.0)
    return jnp.concatenate(outs, axis=0), s_new


def _hgrn_prompt_kernel(z_ref, lb_ref, gw_ref, tri_ref, bd_ref, o_ref, s_ref, s_sc, *, chunk, sub, n_chunks):
    j = pl.program_id(1)

    @pl.when(j == 0)
    def _():
        s_sc[...] = jnp.zeros(s_sc.shape, F32)

    s_olds = [s_sc[p] for p in range(N_HEADS // 2)]
    lb = lb_ref[...]
    rows = []
    for c in range(n_chunks):
        z = z_ref[c * chunk:(c + 1) * chunk, :]
        fg = lb + (1.0 - lb) * jax.nn.sigmoid(z[:, D_REC:2 * D_REC])
        kk = 1.0 - fg
        q = _silu(z[:, :D_REC])
        b = sum(jnp.dot(tri_ref[...], part, preferred_element_type=F32) for part in _split3(jnp.log(fg)))
        outs = []
        for p in range(N_HEADS // 2):
            ps = slice(p * LANES, (p + 1) * LANES)
            o, s_new = _hgrn_pair(q[:, ps], kk[:, ps], b[:, ps],
                                  z[:, 2 * D_REC + p * LANES:2 * D_REC + (p + 1) * LANES],
                                  bd_ref[...], s_olds[p], chunk=chunk, sub=sub)
            gate = _silu(z[:, 3 * D_REC + p * LANES:3 * D_REC + (p + 1) * LANES])
            outs.append(((o * gw_ref[...]) * gate).astype(o_ref.dtype))
            s_olds[p] = s_new
        rows.append(jnp.concatenate(outs, axis=1))
    for p in range(N_HEADS // 2):
        s_sc[p] = s_olds[p]
    o_ref[...] = jnp.concatenate(rows, axis=0)

    @pl.when(j == pl.num_programs(1) - 1)
    def _():
        for h in range(N_HEADS):
            d0 = (h % 2) * D_HEAD
            s_ref[0, h] = s_sc[h // 2][d0:d0 + D_HEAD, d0:d0 + D_HEAD]


def _hgrn_prompt(zrec, lb, gw, batch, t, chunk, sub, n_chunks):
    nc = t // (chunk * n_chunks)
    rows = chunk * n_chunks
    idx = jnp.arange(chunk)
    tri = (idx[:, None] >= idx[None, :]).astype(BF16)
    li = jnp.arange(LANES) // D_HEAD
    bd_mean = ((li[:, None] == li[None, :]).astype(F32) / D_HEAD).astype(BF16)
    gw2 = jnp.concatenate([gw, gw], axis=1)
    return pl.pallas_call(
        functools.partial(_hgrn_prompt_kernel, chunk=chunk, sub=sub, n_chunks=n_chunks),
        grid=(batch, nc),
        in_specs=[pl.BlockSpec((rows, 4 * D_REC), lambda b, j: (b * nc + j, 0)),
                  pl.BlockSpec((1, D_REC), lambda b, j: (0, 0)),
                  pl.BlockSpec((1, LANES), lambda b, j: (0, 0)),
                  pl.BlockSpec((chunk, chunk), lambda b, j: (0, 0)),
                  pl.BlockSpec((LANES, LANES), lambda b, j: (0, 0))],
        out_specs=[pl.BlockSpec((rows, D_REC), lambda b, j: (b * nc + j, 0)),
                   pl.BlockSpec((1, N_HEADS, D_HEAD, D_HEAD), lambda b, j: (b, 0, 0, 0))],
        out_shape=[jax.ShapeDtypeStruct((batch * t, D_REC), BF16),
                   jax.ShapeDtypeStruct((batch, N_HEADS, D_HEAD, D_HEAD), F32)],
        scratch_shapes=[pltpu.VMEM((N_HEADS // 2, LANES, LANES), F32)],
        compiler_params=_params(("parallel", "arbitrary"), 32),
        name="hgrn_prompt",
    )(zrec, lb, gw2, tri, bd_mean)


def _hgrn_sample_kernel(z_ref, s0_ref, lb_ref, gw_ref, tri_ref, o_ref, s_ref, s_sc, o_sc, *, seqs, t, chunk):
    lb = lb_ref[...]
    gw = gw_ref[...]
    tri = tri_ref[...]
    pad = jnp.zeros((chunk - t, D_REC), F32)
    for s in range(seqs):
        z = z_ref[s]
        s_sc[...] = s0_ref[s]
        zq, zf, zi = (jnp.concatenate([z[:, u * D_REC:(u + 1) * D_REC], pad], axis=0) for u in range(3))
        _hgrn_chunk(zq, zf, zi, lb, gw, tri, s_sc, o_sc, chunk=chunk, sub=chunk, n_valid=t)
        o_ref[s] = (o_sc[:t, :] * _silu(z[:, 3 * D_REC:])).astype(o_ref.dtype)
        s_ref[s] = s_sc[...]


def _hgrn_sample(zrec, s0, lb, gw, batch, t, seqs):
    chunk = 8
    idx = jnp.arange(chunk)
    tri = (idx[:, None] >= idx[None, :]).astype(F32)
    return pl.pallas_call(
        functools.partial(_hgrn_sample_kernel, seqs=seqs, t=t, chunk=chunk),
        grid=(batch // seqs,),
        in_specs=[pl.BlockSpec((seqs, t, 4 * D_REC), lambda b: (b, 0, 0)),
                  pl.BlockSpec((seqs, N_HEADS, D_HEAD, D_HEAD), lambda b: (b, 0, 0, 0)),
                  pl.BlockSpec((1, D_REC), lambda b: (0, 0)),
                  pl.BlockSpec((1, D_HEAD), lambda b: (0, 0)),
                  pl.BlockSpec((chunk, chunk), lambda b: (0, 0))],
        out_specs=[pl.BlockSpec((seqs, t, D_REC), lambda b: (b, 0, 0)),
                   pl.BlockSpec((seqs, N_HEADS, D_HEAD, D_HEAD), lambda b: (b, 0, 0, 0))],
        out_shape=[jax.ShapeDtypeStruct((batch, t, D_REC), BF16),
                   jax.ShapeDtypeStruct((batch, N_HEADS, D_HEAD, D_HEAD), F32)],
        scratch_shapes=[pltpu.VMEM((N_HEADS, D_HEAD, D_HEAD), F32), pltpu.VMEM((chunk, D_REC), F32)],
        compiler_params=_params(("parallel",), 32),
        name="hgrn_sample",
    )(zrec.reshape(batch, t, 4 * D_REC), s0, lb, gw, tri)


def _fox_sample_kernel(pt_ref, q_ref, kn_ref, vn_ref, lft_ref, gs_ref, ck_ref, cv_ref, clf_ref, o_ref,
                       kbuf, vbuf, lbuf, sem, *, t, ppb, n_pages):
    b = pl.program_id(0)
    nb = pl.num_programs(0)
    nchunk = n_pages // ppb

    def chunk_copies(bb, c, slot):
        for u in range(ppb):
            page = pt_ref[bb, n_pages - 1 - (c * ppb + u)]
            yield pltpu.make_async_copy(ck_ref.at[page], kbuf.at[slot, u], sem.at[0, slot])
            yield pltpu.make_async_copy(cv_ref.at[page], vbuf.at[slot, u], sem.at[1, slot])
            yield pltpu.make_async_copy(clf_ref.at[page], lbuf.at[slot, u], sem.at[2, slot])

    def start_chunk(bb, c, slot):
        for cp in chunk_copies(bb, c, slot):
            cp.start()

    def wait_chunk(slot):
        pltpu.make_async_copy(ck_ref.at[pl.ds(0, ppb)], kbuf.at[slot], sem.at[0, slot]).wait()
        pltpu.make_async_copy(cv_ref.at[pl.ds(0, ppb)], vbuf.at[slot], sem.at[1, slot]).wait()
        pltpu.make_async_copy(clf_ref.at[pl.ds(0, ppb)], lbuf.at[slot], sem.at[2, slot]).wait()

    @pl.when(b == 0)
    def _():
        start_chunk(0, 0, 0)

    nrow = t * N_HEADS
    lane = lax.broadcasted_iota(jnp.int32, (nrow, D_ATT), 1)
    rowi = lax.broadcasted_iota(jnp.int32, (nrow, D_ATT), 0)
    head_lanes = (lane // D_HEAD) == (rowi % N_HEADS)
    q = q_ref[0]
    qbd = jnp.concatenate([jnp.broadcast_to(q[u:u + 1, :], (N_HEADS, D_ATT)) for u in range(t)], axis=0)
    qbd = jnp.where(head_lanes, qbd, jnp.zeros_like(qbd))

    lft = lft_ref[0]
    cols = [lft[:, 0:1]]
    for u in range(1, t):
        cols.append(cols[-1] + lft[:, u:u + 1])
    cq_t = jnp.concatenate(cols, axis=1)
    cq_col = jnp.concatenate(cols, axis=0)
    zpad = jnp.zeros((N_HEADS - t, D_ATT), F32)
    kn = jnp.concatenate([kn_ref[0], zpad], axis=0).astype(BF16)
    vn = jnp.concatenate([vn_ref[0], zpad], axis=0).astype(BF16)
    cq_t8 = jnp.concatenate([cq_t, jnp.zeros((N_HEADS, N_HEADS - t), F32)], axis=1)
    s = lax.dot_general(qbd, kn, NT_DIMS, preferred_element_type=F32)
    s = s + cq_col - jnp.concatenate([cq_t8] * t, axis=0)
    r2 = lax.broadcasted_iota(jnp.int32, (nrow, N_HEADS), 0) // N_HEADS
    c2 = lax.broadcasted_iota(jnp.int32, (nrow, N_HEADS), 1)
    s = jnp.where(c2 <= r2, s, NEG_BIG)
    m = jnp.max(s, axis=-1, keepdims=True)
    p0 = jnp.exp(s - m)
    l = jnp.sum(p0, axis=-1, keepdims=True)
    acc = jnp.dot(p0.astype(BF16), vn, preferred_element_type=F32)
    carry = jnp.zeros((N_HEADS, 1), F32)

    for c in range(nchunk):
        slot = (b * nchunk + c) % 2
        if c + 1 < nchunk:
            start_chunk(b, c + 1, 1 - slot)
        else:
            @pl.when(b + 1 < nb)
            def _():
                start_chunk(b + 1, 0, 1 - slot)
        wait_chunk(slot)
        lf_pages = [lbuf[slot, u] for u in range(ppb)]
        insuf = jnp.dot(jnp.concatenate(lf_pages, axis=0), gs_ref[...], precision=HIGHEST,
                        preferred_element_type=F32)
        biases = []
        for u in range(ppb):
            biases.append(insuf[u * N_HEADS:(u + 1) * N_HEADS, :] + carry)
            carry = carry + jnp.sum(lf_pages[u], axis=-1, keepdims=True)
        bias = jnp.concatenate(biases, axis=1)
        bias = jnp.concatenate([bias] * t, axis=0) + cq_col
        kcat = jnp.concatenate([kbuf[slot, u].astype(BF16) for u in range(ppb)], axis=1)
        vcat = jnp.concatenate([vbuf[slot, u].astype(BF16) for u in range(ppb)], axis=1)
        s = jnp.dot(qbd, kcat, preferred_element_type=F32) + bias
        m_new = jnp.maximum(m, jnp.max(s, axis=-1, keepdims=True))
        alpha = jnp.exp(m - m_new)
        pr = jnp.exp(s - m_new)
        l = alpha * l + jnp.sum(pr, axis=-1, keepdims=True)
        acc = alpha * acc + lax.dot_general(pr.astype(BF16), vcat, NT_DIMS, preferred_element_type=F32)
        m = m_new

    o = jnp.where(head_lanes, acc / l, 0.0)
    o_ref[0] = jnp.sum(o.reshape(t, N_HEADS, D_ATT), axis=1).astype(o_ref.dtype)


def _fox_sample(page_table, q, kn, vn, lft, cache_k, cache_v, cache_lft, batch, t, ppb):
    n_pages = page_table.shape[1]
    ppb = min(ppb, n_pages)
    idx = jnp.arange(PAGE_SIZE)
    gs = (idx[:, None] > idx[None, :]).astype(F32)
    seq = lambda w: pl.BlockSpec((1, t, w), lambda b, pt: (b, 0, 0))
    hbm = pl.BlockSpec(memory_space=pl.ANY)
    grid_spec = pltpu.PrefetchScalarGridSpec(
        num_scalar_prefetch=1,
        grid=(batch,),
        in_specs=[seq(D_ATT), seq(D_ATT), seq(D_ATT),
                  pl.BlockSpec((1, N_HEADS, t), lambda b, pt: (b, 0, 0)),
                  pl.BlockSpec((PAGE_SIZE, PAGE_SIZE), lambda b, pt: (0, 0)),
                  hbm, hbm, hbm],
        out_specs=pl.BlockSpec((1, t, D_ATT), lambda b, pt: (b, 0, 0)),
        scratch_shapes=[pltpu.VMEM((2, ppb, D_ATT, PAGE_SIZE), F32), pltpu.VMEM((2, ppb, D_ATT, PAGE_SIZE), F32),
                        pltpu.VMEM((2, ppb, N_HEADS, PAGE_SIZE), F32), pltpu.SemaphoreType.DMA((3, 2))],
    )
    return pl.pallas_call(
        functools.partial(_fox_sample_kernel, t=t, ppb=ppb, n_pages=n_pages),
        grid_spec=grid_spec,
        out_shape=jax.ShapeDtypeStruct((batch, t, D_ATT), BF16),
        compiler_params=_params(("arbitrary",), 48),
        name="fox_sample",
    )(page_table, q.reshape(batch, t, D_ATT), kn.reshape(batch, t, D_ATT), vn.reshape(batch, t, D_ATT), lft, gs,
      cache_k, cache_v, cache_lft)


def _outproj_router_kernel(x_ref, orec_ref, oatt_ref, wo_ref, nw_ref, wr_ref, br_ref, cnt0_ref, tril_ref,
                           h_ref, xn_ref, idx_ref, gate_ref, rank_ref, cnt_ref, base_sc):
    i = pl.program_id(0)

    @pl.when(i == 0)
    def _():
        base_sc[...] = cnt0_ref[...]

    tm = x_ref.shape[0]
    h = (x_ref[...]
         + jnp.dot(orec_ref[...], wo_ref[:D_REC, :], preferred_element_type=F32)
         + jnp.dot(oatt_ref[...], wo_ref[D_REC:, :], preferred_element_type=F32))
    h_ref[...] = h
    r = lax.rsqrt(jnp.mean(h * h, axis=-1, keepdims=True) + RMS_EPS)
    xn = (h * r) * nw_ref[...]
    xn_ref[...] = xn
    x_hi = xn.astype(BF16)
    x_lo = (xn - x_hi.astype(F32)).astype(BF16)
    logits = (jnp.dot(x_hi, wr_ref[0], preferred_element_type=F32)
              + (jnp.dot(x_lo, wr_ref[0], preferred_element_type=F32)
                 + jnp.dot(x_hi, wr_ref[1], preferred_element_type=F32))) + br_ref[...]
    lane = lax.broadcasted_iota(jnp.int32, (tm, LANES), 1)
    vals, hots = [], []
    idx_out = jnp.zeros((tm, LANES), jnp.int32)
    for k in range(TOP_K):
        m = jnp.max(logits, axis=-1, keepdims=True)
        sel = jnp.min(jnp.where(logits == m, lane, LANES), axis=-1, keepdims=True)
        hot = lane == sel
        vals.append(m)
        hots.append(hot)
        idx_out = jnp.where(lane == k, sel, idx_out)
        logits = jnp.where(hot, -jnp.inf, logits)
    idx_ref[...] = idx_out
    es = [jnp.exp(v - vals[0]) for v in vals]
    denom = es[0] + es[1] + es[2] + es[3]
    gate_out = jnp.zeros((tm, LANES), F32)
    for k in range(TOP_K):
        gate_out = jnp.where(lane == k, es[k] / denom, gate_out)
    gate_ref[...] = gate_out
    base = base_sc[...]
    rank_out = jnp.zeros((tm, LANES), jnp.int32)
    for k in range(TOP_K):
        hot_f = hots[k].astype(F32)
        before = jnp.dot(tril_ref[...], hot_f.astype(BF16), preferred_element_type=F32) + base
        rank = jnp.sum(jnp.where(hots[k], before, 0.0), axis=-1, keepdims=True)
        rank_out = jnp.where(lane == k, rank.astype(jnp.int32), rank_out)
        base = base + jnp.sum(hot_f, axis=0, keepdims=True)
    rank_ref[...] = rank_out
    base_sc[...] = base
    cnt_ref[...] = base


def _outproj_router(x, orec, oatt, wo, nw, wr, br, cnt0, tm):
    n = x.shape[0]
    idx = jnp.arange(tm)
    tril = (idx[:, None] > idx[None, :]).astype(BF16)
    full = lambda a: pl.BlockSpec(a.shape, lambda i: (0,) * a.ndim)
    row = lambda w: pl.BlockSpec((tm, w), lambda i: (i, 0))
    return pl.pallas_call(
        _outproj_router_kernel,
        grid=(n // tm,),
        in_specs=[row(D_MODEL), row(D_REC), row(D_ATT), full(wo), full(nw), full(wr), full(br), full(cnt0), full(tril)],
        out_specs=[row(D_MODEL), row(D_MODEL), row(LANES), row(LANES), row(LANES),
                   pl.BlockSpec((1, LANES), lambda i: (0, 0))],
        out_shape=[jax.ShapeDtypeStruct((n, D_MODEL), F32), jax.ShapeDtypeStruct((n, D_MODEL), F32),
                   jax.ShapeDtypeStruct((n, LANES), jnp.int32), jax.ShapeDtypeStruct((n, LANES), F32),
                   jax.ShapeDtypeStruct((n, LANES), jnp.int32), jax.ShapeDtypeStruct((1, LANES), F32)],
        scratch_shapes=[pltpu.VMEM((1, LANES), F32)],
        compiler_params=_params(("arbitrary",), 40),
        name="outproj_router",
    )(x, orec, oatt, wo, nw, wr, br, cnt0, tril)


def _dispatch_kernel(gend_ref, gsize_ref, total_ref, dest_ref, xp_ref, xs_ref, out_ref, zbuf, sem, zsem,
                     *, n_prompt_tiles, tile_rows):
    i = pl.program_id(0)
    n_tiles = out_ref.shape[0] // tile_rows

    @pl.when(i == 0)
    def _():
        zbuf[...] = jnp.zeros(zbuf.shape, zbuf.dtype)

        def zero_tile(start):
            cp = pltpu.make_async_copy(zbuf, out_ref.at[pl.ds(pl.multiple_of(start, tile_rows), tile_rows), :], zsem)
            cp.start()
            cp.wait()

        for e in range(N_EXPERTS):
            @pl.when(gsize_ref[e] > 0)
            def _():
                zero_tile(gend_ref[e] - tile_rows)

        def tail(tt, carry):
            zero_tile(tt * tile_rows)
            return carry

        lax.fori_loop(total_ref[0] // tile_rows, n_tiles, tail, 0)

    def scatter(src_ref):
        n = src_ref.shape[0] * SUBLANES

        def issue(g, carry):
            for u in range(SUBLANES):
                for k in range(TOP_K):
                    pltpu.make_async_copy(src_ref.at[g, pl.ds(u, 1), :],
                                          out_ref.at[pl.ds(dest_ref[0, 0, (g * SUBLANES + u) * TOP_K + k], 1), :],
                                          sem).start(priority=k % 2)
            return carry

        lax.fori_loop(0, n // SUBLANES, issue, 0)
        for k in range(TOP_K):
            pltpu.make_async_copy(zbuf.at[pl.ds(0, n), :], out_ref.at[pl.ds(0, n), :], sem).wait()

    @pl.when(i < n_prompt_tiles)
    def _():
        scatter(xp_ref)

    @pl.when(i == n_prompt_tiles)
    def _():
        scatter(xs_ref)


def _dispatch(group_end, group_size, dest_p, dest_s, xn_p, xn_s, n_rows, tm, tile_rows):
    n_p, n_s = xn_p.shape[0], xn_s.shape[0]
    npt = n_p // tm
    dest = jnp.concatenate([dest_p.reshape(npt, tm * TOP_K),
                            jnp.pad(dest_s, (0, (tm - n_s) * TOP_K)).reshape(1, tm * TOP_K)], axis=0)
    grid_spec = pltpu.PrefetchScalarGridSpec(
        num_scalar_prefetch=3,
        grid=(npt + 1,),
        in_specs=[pl.BlockSpec((1, 1, tm * TOP_K), lambda i, *_: (i, 0, 0), memory_space=pltpu.SMEM),
                  pl.BlockSpec((tm // SUBLANES, SUBLANES, D_MODEL), lambda i, *_: (jnp.minimum(i, npt - 1), 0, 0)),
                  pl.BlockSpec((n_s // SUBLANES, SUBLANES, D_MODEL), lambda i, *_: (0, 0, 0))],
        out_specs=pl.BlockSpec(memory_space=pl.ANY),
        scratch_shapes=[pltpu.VMEM((tile_rows, D_MODEL), F32), pltpu.SemaphoreType.DMA(()),
                        pltpu.SemaphoreType.DMA(())],
    )
    return pl.pallas_call(
        functools.partial(_dispatch_kernel, n_prompt_tiles=npt, tile_rows=tile_rows),
        grid_spec=grid_spec,
        out_shape=jax.ShapeDtypeStruct((n_rows, D_MODEL), F32),
        compiler_params=_params(("arbitrary",), 32),
        name="moe_dispatch",
    )(group_end, group_size, group_end[-1:], dest.reshape(npt + 1, 1, tm * TOP_K),
      xn_p.reshape(n_p // SUBLANES, SUBLANES, D_MODEL), xn_s.reshape(n_s // SUBLANES, SUBLANES, D_MODEL))


def _expert_mlp_kernel(te_ref, tv_ref, tf_ref, x_ref, wg_ref, bg_ref, wu_ref, bu_ref, wd_ref, bd_ref, y_ref,
                       wg_sc, wu_sc, wd_sc):
    i = pl.program_id(0)
    del te_ref

    @pl.when(tf_ref[i] != 0)
    def _():
        wg_sc[...] = wg_ref[0].astype(BF16)
        wu_sc[...] = wu_ref[0].astype(BF16)
        wd_sc[...] = wd_ref[0].astype(BF16)

    @pl.when(tv_ref[i] != 0)
    def _():
        x = x_ref[...].astype(BF16)
        gate = jnp.minimum(jnp.dot(x, wg_sc[...], preferred_element_type=F32) + bg_ref[0], SWIGLU_LIMIT)
        up = jnp.clip(jnp.dot(x, wu_sc[...], preferred_element_type=F32) + bu_ref[0], -SWIGLU_LIMIT, SWIGLU_LIMIT)
        glu = gate * jax.nn.sigmoid(gate * SWIGLU_ALPHA)
        mid = ((up + 1.0) * glu).astype(BF16)
        y_ref[...] = jnp.dot(mid, wd_sc[...], preferred_element_type=F32) + bd_ref[0]

    @pl.when(tv_ref[i] == 0)
    def _():
        y_ref[...] = jnp.zeros(y_ref.shape, y_ref.dtype)


def _expert_mlp(tile_expert, tile_valid, tile_first, xs, wg, bg, wu, bu, wd, bd, tm):
    p = xs.shape[0]
    wspec = lambda: pl.BlockSpec((1, D_MODEL, D_FF), lambda i, te, tv, tf: (te[i], 0, 0))
    bspec = lambda: pl.BlockSpec((1, 1, D_FF), lambda i, te, tv, tf: (te[i], 0, 0))
    grid_spec = pltpu.PrefetchScalarGridSpec(
        num_scalar_prefetch=3,
        grid=(p // tm,),
        in_specs=[pl.BlockSpec((tm, D_MODEL), lambda i, te, tv, tf: (i * tv[i], 0)),
                  wspec(), bspec(), wspec(), bspec(), wspec(), bspec()],
        out_specs=pl.BlockSpec((tm, D_MODEL), lambda i, te, tv, tf: (i, 0)),
        scratch_shapes=[pltpu.VMEM((D_MODEL, D_FF), BF16), pltpu.VMEM((D_MODEL, D_FF), BF16),
                        pltpu.VMEM((D_FF, D_MODEL), BF16)],
    )
    return pl.pallas_call(
        _expert_mlp_kernel,
        grid_spec=grid_spec,
        out_shape=jax.ShapeDtypeStruct((p, D_MODEL), F32),
        compiler_params=_params(("arbitrary",), 56),
        name="expert_mlp",
    )(tile_expert, tile_valid, tile_first, xs, wg, bg, wu, bu, wd, bd)


def _combine_kernel(dest_ref, dest_next_ref, gate_ref, h_ref, nw_ref, ys_ref, y_ref, buf, sem):
    i = pl.program_id(0)
    tm = h_ref.shape[0]
    slot = i % 2

    def request(dref, s):
        def issue(g, carry):
            for u in range(SUBLANES):
                for k in range(TOP_K):
                    pltpu.make_async_copy(ys_ref.at[pl.ds(dref[0, 0, (g * SUBLANES + u) * TOP_K + k], 1), :],
                                          buf.at[s, k, g, pl.ds(u, 1), :], sem.at[s]).start(priority=k % 2)
            return carry

        lax.fori_loop(0, tm // SUBLANES, issue, 0)

    @pl.when(i == 0)
    def _():
        request(dest_ref, 0)

    for s in range(2):
        @pl.when(slot == s)
        def _():
            @pl.when(i + 1 < pl.num_programs(0))
            def _():
                request(dest_next_ref, 1 - s)

            for k in range(TOP_K):
                pltpu.make_async_copy(ys_ref.at[pl.ds(0, tm), :], y_ref, sem.at[s]).wait()

    gates = gate_ref[...]
    moe = gates[:, 0:1] * buf[slot, 0].reshape(tm, D_MODEL)
    for k in range(1, TOP_K):
        moe = moe + gates[:, k:k + 1] * buf[slot, k].reshape(tm, D_MODEL)
    h = h_ref[...] + moe
    r = lax.rsqrt(jnp.mean(h * h, axis=-1, keepdims=True) + RMS_EPS)
    y_ref[...] = (h * r) * nw_ref[...]


def _combine(dest, gates, h, nw, ys, tm):
    n = h.shape[0]
    nt = n // tm
    dest3 = dest.reshape(nt, 1, tm * TOP_K)
    return pl.pallas_call(
        _combine_kernel,
        grid=(nt,),
        in_specs=[pl.BlockSpec((1, 1, tm * TOP_K), lambda i: (i, 0, 0), memory_space=pltpu.SMEM),
                  pl.BlockSpec((1, 1, tm * TOP_K), lambda i: (jnp.minimum(i + 1, nt - 1), 0, 0),
                               memory_space=pltpu.SMEM),
                  pl.BlockSpec((tm, LANES), lambda i: (i, 0)),
                  pl.BlockSpec((tm, D_MODEL), lambda i: (i, 0)),
                  pl.BlockSpec((1, D_MODEL), lambda i: (0, 0)),
                  pl.BlockSpec(memory_space=pl.ANY)],
        out_specs=pl.BlockSpec((tm, D_MODEL), lambda i: (i, 0)),
        out_shape=jax.ShapeDtypeStruct((n, D_MODEL), F32),
        scratch_shapes=[pltpu.VMEM((2, TOP_K, tm // SUBLANES, SUBLANES, D_MODEL), F32),
                        pltpu.SemaphoreType.DMA((2,))],
        compiler_params=_params(("arbitrary",), 32),
        name="moe_combine",
    )(dest3, dest3, gates, h, nw, ys)


TM_TOKENS = 512
TM_EXPERT = 512
TM_COMBINE = 256
TM_DISPATCH = 1024
TQ_PROMPT = 512
TK_PROMPT = 512
HGRN_CHUNK = 128
HGRN_SUB = 32
HGRN_CHUNKS_PER_STEP = 8
PAGES_PER_STEP = 16
HGRN_SAMPLE_SEQS = 8


def _step(x_prompt, x_sample, cache_k, cache_v, cache_logf, state_hgrn, page_table,
          norm_mix_w, w_in, b_fa, lb_param, gnorm_w, w_out, norm_ffn_w,
          w_router, b_router, w_gate, b_gate, w_up, b_up, w_down, b_down, norm_final_w):
    batch, t, _ = x_prompt.shape
    dec_batch, dec_t, _ = x_sample.shape
    n_p, n_s = batch * t, dec_batch * dec_t
    n_pool = cache_k.shape[1]

    w_in_b = w_in[0].astype(BF16)
    wrec = w_in_b[:, :4 * D_REC]
    watt = w_in_b[:, 4 * D_REC:4 * D_REC + 3 * D_ATT]
    wfa = jnp.pad(w_in_b[:, 4 * D_REC + 3 * D_ATT:], ((0, 0), (0, LANES - N_HEADS)))
    wfat = jnp.pad(w_in_b[:, 4 * D_REC + 3 * D_ATT:].T, ((0, 16 - N_HEADS), (0, 0)))
    bfa = b_fa[0].reshape(1, N_HEADS)
    bfat = b_fa[0].reshape(N_HEADS, 1)
    nw_mix = norm_mix_w[0].reshape(1, D_MODEL)
    nw_ffn = norm_ffn_w[0].reshape(1, D_MODEL)
    nw_fin = norm_final_w.reshape(1, D_MODEL)
    lb = jnp.cumsum(jax.nn.softmax(lb_param.astype(F32), axis=0), axis=0)[0].reshape(1, D_REC)
    gw = gnorm_w.reshape(1, D_HEAD)
    wo = w_out[0].astype(BF16)
    wr_f = jnp.pad(w_router[0], ((0, 0), (0, LANES - N_EXPERTS)))
    wr_hi = wr_f.astype(BF16)
    wr = jnp.stack([wr_hi, (wr_f - wr_hi.astype(F32)).astype(BF16)])
    br = jnp.pad(b_router[0], (0, LANES - N_EXPERTS), constant_values=NEG_BIG).reshape(1, LANES)
    wg, wu, wd = w_gate[0], w_up[0], w_down[0]
    bg, bu, bd = (b[0].reshape(N_EXPERTS, 1, -1) for b in (b_gate, b_up, b_down))

    xp = x_prompt.reshape(n_p, D_MODEL)
    xs = x_sample.reshape(n_s, D_MODEL)
    tm_p, tm_s = min(TM_TOKENS, n_p), min(TM_TOKENS, n_s)
    proj = functools.partial(_inproj, nw=nw_mix, wrec=wrec, watt=watt, wfa=wfa, wfat=wfat, bfa=bfa, bfat=bfat)
    zrec_p, q_p, kt_p, vt_p, kb_p, vb_p, lf_p, lft_p = proj(xp, tm=tm_p, seq_len=t)
    zrec_s, q_s, k_s, v_s, _, _, lf_s, lft_s = proj(xs, tm=tm_s)

    c_p, ct_p = _cumsum_logf(lf_p, lft_p, batch, t)
    oatt_p = _fox_prompt(q_p, kb_p, vb_p, c_p, ct_p, batch, t, TQ_PROMPT, TK_PROMPT)
    orec_p, state_p = _hgrn_prompt(zrec_p, lb, gw, batch, t, HGRN_CHUNK, HGRN_SUB,
                                   min(HGRN_CHUNKS_PER_STEP, t // HGRN_CHUNK))

    orec_s, state_s = _hgrn_sample(zrec_s, state_hgrn[0], lb, gw, dec_batch, dec_t, HGRN_SAMPLE_SEQS)
    lft_s3 = lft_s.reshape(N_HEADS, dec_batch, dec_t).transpose(1, 0, 2)
    cache_lft = cache_logf[0].transpose(0, 2, 1)
    oatt_s = _fox_sample(page_table, q_s, k_s, v_s, lft_s3,
                         cache_k[0].transpose(0, 2, 3, 1).reshape(n_pool, D_ATT, PAGE_SIZE),
                         cache_v[0].transpose(0, 2, 3, 1).reshape(n_pool, D_ATT, PAGE_SIZE),
                         cache_lft, dec_batch, dec_t, PAGES_PER_STEP)

    route = functools.partial(_outproj_router, wo=wo, nw=nw_ffn, wr=wr, br=br)
    h_p, xn_p, idx_p, gate_p, rank_p, cnt_p = route(xp, orec_p, oatt_p, cnt0=jnp.zeros((1, LANES), F32), tm=tm_p)
    h_s, xn_s, idx_s, gate_s, rank_s, cnt = route(xs, orec_s.reshape(n_s, D_REC), oatt_s.reshape(n_s, D_ATT),
                                                  cnt0=cnt_p, tm=tm_s)

    counts = cnt[0, :N_EXPERTS].astype(jnp.int32)
    padded = ((counts + TM_EXPERT - 1) // TM_EXPERT) * TM_EXPERT
    group_end = jnp.cumsum(padded)
    group_start = group_end - padded
    n_rows = (n_p + n_s) * TOP_K + N_EXPERTS * TM_EXPERT
    tile_start = jnp.arange(n_rows // TM_EXPERT, dtype=jnp.int32) * TM_EXPERT
    tile_expert = jnp.minimum(jnp.sum((tile_start[:, None] >= group_end[None, :]).astype(jnp.int32), axis=1),
                              N_EXPERTS - 1)
    tile_valid = (tile_start < group_end[-1]).astype(jnp.int32)
    tile_first = jnp.concatenate([jnp.ones((1,), jnp.int32),
                                  (tile_expert[1:] != tile_expert[:-1]).astype(jnp.int32)])
    dest_p = (group_start[idx_p[:, :TOP_K]] + rank_p[:, :TOP_K]).reshape(-1)
    dest_s = (group_start[idx_s[:, :TOP_K]] + rank_s[:, :TOP_K]).reshape(-1)

    x_sorted = _dispatch(group_end.astype(jnp.int32), padded, dest_p, dest_s, xn_p, xn_s, n_rows,
                         min(TM_DISPATCH, n_p), TM_EXPERT)
    y_sorted = _expert_mlp(tile_expert, tile_valid, tile_first, x_sorted, wg, bg, wu, bu, wd, bd, TM_EXPERT)
    y_p = _combine(dest_p, gate_p, h_p, nw_fin, y_sorted, min(TM_COMBINE, n_p))
    y_s = _combine(dest_s, gate_s, h_s, nw_fin, y_sorted, n_s)

    heads = lambda a, b_, t_: a.reshape(1, b_, t_, N_HEADS, D_HEAD)
    heads_t = lambda a: a.reshape(1, batch, N_HEADS, D_HEAD, t).transpose(0, 1, 4, 2, 3)
    return (y_p.reshape(batch, t, D_MODEL), y_s.reshape(dec_batch, dec_t, D_MODEL),
            heads_t(kt_p), heads_t(vt_p), lf_p.reshape(1, batch, t, N_HEADS), state_p[None],
            heads(k_s, dec_batch, dec_t), heads(v_s, dec_batch, dec_t), lf_s.reshape(1, dec_batch, dec_t, N_HEADS),
            state_s[None])


def kernel(x_prompt, x_sample, cache_k, cache_v, cache_logf, state_hgrn, page_table, norm_mix_w, w_in, b_fa, lb_param, gnorm_w, w_out, norm_ffn_w, w_router, b_router, w_gate, b_gate, w_up, b_up, w_down, b_down, norm_final_w):
    return _step(x_prompt, x_sample, cache_k, cache_v, cache_logf, state_hgrn, page_table,
                 norm_mix_w, w_in, b_fa, lb_param, gnorm_w, w_out, norm_ffn_w,
                 w_router, b_router, w_gate, b_gate, w_up, b_up, w_down, b_down, norm_final_w)
```
